```python
import math
import jax
import jax.numpy as jnp
from jax import lax
import numpy as np

D_MODEL = 1024
BATCH = 4
SEQ = 4096
DEPTH = 4

N_MIXERS = 4
GROUP_W = D_MODEL // N_MIXERS
HEAD_DIM = 64
N_HEADS = GROUP_W // HEAD_DIM
S5_GROUP_CH = 16
S5_GROUPS = GROUP_W // S5_GROUP_CH
S5_STATE = 64
RET_CHUNK = 128
ROPE_BASE = 10000.0
RWKV_LORA_W = 64
RWKV_LORA_A = 64
RWKV_LORA_V = 32
RWKV_LORA_G = 128
RWKV_GN_EPS = 64e-5
SB_BLOCK = 128
D_FF = 4 * D_MODEL
NORM_EPS = 1e-6

OFF_S5 = 0
OFF_RET = OFF_S5 + GROUP_W
OFF_SB = OFF_RET + 4 * GROUP_W
OFF_RW = OFF_SB + 3 * GROUP_W
N_RW0 = 3 * GROUP_W + RWKV_LORA_W + RWKV_LORA_A + RWKV_LORA_G
N_IN0 = OFF_RW + N_RW0
N_IN = N_IN0 + RWKV_LORA_V

kernel_name = "hybrid_s5_retnet_rwkv7_stickbreak_trunk"


def rmsnorm(x, gain):
    xf = x.astype(jnp.float32)
    y = xf * lax.rsqrt(jnp.mean(xf * xf, axis=-1, keepdims=True) + NORM_EPS)
    return (y * gain.astype(jnp.float32)).astype(x.dtype)


def token_shift(t):
    return jnp.pad(t[:, :-1], ((0, 0), (1, 0), (0, 0)))


def rope_tables(seq):
    inv_freq = ROPE_BASE ** (-jnp.arange(0, HEAD_DIM, 2, dtype=jnp.float32) / HEAD_DIM)
    ang = jnp.arange(seq, dtype=jnp.float32)[:, None] * inv_freq[None, :]
    return jnp.cos(ang), jnp.sin(ang)


def apply_rope(t, cos, sin):
    t1, t2 = jnp.split(t.astype(jnp.float32), 2, axis=-1)
    c = cos[None, :, None, :]
    s = sin[None, :, None, :]
    return jnp.concatenate([t1 * c - t2 * s, t1 * s + t2 * c], axis=-1)


def _complex_affine_combine(e1, e2):
    a1r, a1i, b1r, b1i = e1
    a2r, a2i, b2r, b2i = e2
    return (a2r * a1r - a2i * a1i,
            a2r * a1i + a2i * a1r,
            a2r * b1r - a2i * b1i + b2r,
            a2r * b1i + a2i * b1r + b2i)


def s5_mixer(u, a_re, a_im, log_dt, b_re, b_im, c_re, c_im, d_skip, glu_w1, glu_w2):
    f32 = jnp.float32
    bsz, seq, _ = u.shape
    uf = u.astype(f32).reshape(bsz, seq, S5_GROUPS, S5_GROUP_CH)
    a_re = a_re.astype(f32)
    a_im = a_im.astype(f32)
    dt = jnp.exp(log_dt.astype(f32))[:, None]
    mag = jnp.exp(a_re * dt)
    ab_re = mag * jnp.cos(a_im * dt)
    ab_im = mag * jnp.sin(a_im * dt)
    den = a_re * a_re + a_im * a_im
    num_re = ab_re - 1.0
    zoh_re = (num_re * a_re + ab_im * a_im) / den
    zoh_im = (ab_im * a_re - num_re * a_im) / den
    b_re = b_re.astype(f32)
    b_im = b_im.astype(f32)
    bb_re = zoh_re[..., None] * b_re - zoh_im[..., None] * b_im
    bb_im = zoh_re[..., None] * b_im + zoh_im[..., None] * b_re
    bu_re = jnp.einsum("bsgc,gpc->bsgp", uf, bb_re)
    bu_im = jnp.einsum("bsgc,gpc->bsgp", uf, bb_im)
    a_re_t = jnp.broadcast_to(ab_re, bu_re.shape)
    a_im_t = jnp.broadcast_to(ab_im, bu_re.shape)
    _, _, x_re, x_im = lax.associative_scan(
        _complex_affine_combine, (a_re_t, a_im_t, bu_re, bu_im), axis=1)
    y = (jnp.einsum("bsgp,gcp->bsgc", x_re, c_re.astype(f32))
         - jnp.einsum("bsgp,gcp->bsgc", x_im, c_im.astype(f32))
         + d_skip.astype(f32).reshape(S5_GROUPS, S5_GROUP_CH) * uf)
    y = jax.nn.gelu(y.reshape(bsz, seq, GROUP_W)).astype(u.dtype)
    return ((y @ glu_w1) * jax.nn.sigmoid(y @ glu_w2)).astype(u.dtype)


def retention_mixer(q, k, v, g, cos, sin):
    f32 = jnp.float32
    bsz, seq, _ = q.shape
    n_chunks = seq // RET_CHUNK
    q = apply_rope(q.reshape(bsz, seq, N_HEADS, HEAD_DIM), cos, sin)
    k = apply_rope(k.reshape(bsz, seq, N_HEADS, HEAD_DIM), cos, sin) * HEAD_DIM ** -0.5
    v = v.reshape(bsz, seq, N_HEADS, HEAD_DIM).astype(f32)
    log_gamma = jnp.log1p(-jnp.exp2(-5.0 - jnp.arange(N_HEADS, dtype=f32)))
    idx = jnp.arange(RET_CHUNK, dtype=f32)
    rel = idx[:, None] - idx[None, :]
    intra = jnp.where(rel >= 0, jnp.exp(log_gamma[:, None, None] * jnp.maximum(rel, 0.0)), 0.0)
    qc = q.reshape(bsz, n_chunks, RET_CHUNK, N_HEADS, HEAD_DIM)
    kc = k.reshape(bsz, n_chunks, RET_CHUNK, N_HEADS, HEAD_DIM)
    vc = v.reshape(bsz, n_chunks, RET_CHUNK, N_HEADS, HEAD_DIM)
    scores = jnp.einsum("bnihd,bnjhd->bnhij", qc, kc) * intra
    o_intra = jnp.einsum("bnhij,bnjhd->bnihd", scores, vc)
    k_decay = jnp.exp(log_gamma[None, :] * (RET_CHUNK - 1.0 - idx)[:, None])
    chunk_kv = jnp.einsum("bnjhd,jh,bnjhe->nbhde", kc, k_decay, vc)
    chunk_decay = jnp.exp(log_gamma * RET_CHUNK)[None, :, None, None]

    def step(state, kv):
        return chunk_decay * state + kv, state

    _, prev_states = lax.scan(
        step, jnp.zeros((bsz, N_HEADS, HEAD_DIM, HEAD_DIM), f32), chunk_kv)
    q_decay = jnp.exp(log_gamma[None, :] * (idx + 1.0)[:, None])
    o_cross = jnp.einsum("bnihd,ih,nbhde->bnihe", qc, q_decay, prev_states)
    o = (o_intra + o_cross).reshape(bsz, seq, N_HEADS, HEAD_DIM)
    o = o * lax.rsqrt(jnp.mean(o * o, axis=-1, keepdims=True) + NORM_EPS)
    return (jax.nn.silu(g.astype(f32)) * o.reshape(bsz, seq, GROUP_W)).astype(g.dtype)


def rwkv7_mixer(cols, vres_cols, v_first, mu, vres_mu, w0, w2, a0, a2, g2,
                v0, v2, k_k, k_a, r_k, ln_w, ln_b):
    f32 = jnp.float32
    bsz, seq, _ = cols.shape
    xs = cols + (token_shift(cols) - cols) * mu
    r, k, v, wd, ad, gd = jnp.split(
        xs, [GROUP_W, 2 * GROUP_W, 3 * GROUP_W, 3 * GROUP_W + RWKV_LORA_W,
             3 * GROUP_W + RWKV_LORA_W + RWKV_LORA_A], axis=-1)
    w_log = -jax.nn.softplus(-(w0 + jnp.tanh(wd) @ w2)) - 0.5
    decay = jnp.exp(-jnp.exp(w_log.astype(f32)))
    a = jax.nn.sigmoid(a0 + ad @ a2)
    g = jax.nn.sigmoid(gd) @ g2
    if v_first is None:
        v_first = v
    else:
        vx = vres_cols + (token_shift(vres_cols) - vres_cols) * vres_mu
        v = v + (v_first - v) * jax.nn.sigmoid(v0 + vx @ v2)

    def heads(t):
        return t.astype(f32).reshape(bsz, seq, N_HEADS, HEAD_DIM)

    kk = heads(k * k_k)
    kk = kk * lax.rsqrt(jnp.maximum(jnp.sum(kk * kk, axis=-1, keepdims=True), 1e-12))
    k = k * (1.0 + (a - 1.0) * k_a)
    r_h, k_h, v_h, a_h, w_h = heads(r), heads(k), heads(v), heads(a), heads(decay)

    def step(state, inp):
        r_t, w_t, k_t, v_t, aa_t, bb_t = inp
        sa = jnp.einsum("bhvk,bhk->bhv", state, aa_t)
        state = (state * w_t[:, :, None, :] + sa[..., None] * bb_t[:, :, None, :]
                 + v_t[..., None] * k_t[:, :, None, :])
        return state, jnp.einsum("bhvk,bhk->bhv", state, r_t)

    scan_in = tuple(jnp.moveaxis(t, 1, 0) for t in (r_h, w_h, k_h, v_h, -kk, kk * a_h))
    _, ys = lax.scan(step, jnp.zeros((bsz, N_HEADS, HEAD_DIM, HEAD_DIM), f32), scan_in)
    y = jnp.moveaxis(ys, 0, 1)
    mean = jnp.mean(y, axis=-1, keepdims=True)
    var = jnp.mean(jnp.square(y - mean), axis=-1, keepdims=True)
    y = ((y - mean) * lax.rsqrt(var + RWKV_GN_EPS)).reshape(bsz, seq, GROUP_W)
    y = y * ln_w.astype(f32) + ln_b.astype(f32)
    bonus = jnp.sum(r_h * k_h * r_k.astype(f32).reshape(N_HEADS, HEAD_DIM), axis=-1, keepdims=True) * v_h
    out = (y + bonus.reshape(bsz, seq, GROUP_W)) * g.astype(f32)
    return out.astype(cols.dtype), v_first


def stick_breaking_mixer(q, k, v):
    f32 = jnp.float32
    bsz, seq, _ = q.shape
    q = q.reshape(bsz, seq, N_HEADS, HEAD_DIM) * HEAD_DIM ** -0.5
    k = k.reshape(bsz, seq, N_HEADS, HEAD_DIM)
    v = v.reshape(bsz, seq, N_HEADS, HEAD_DIM)
    key_pos = jnp.arange(seq)

    def block(i):
        start = i * SB_BLOCK
        qb = lax.dynamic_slice_in_dim(q, start, SB_BLOCK, axis=1)
        z = jnp.einsum("bqhd,bkhd->bhqk", qb, k).astype(f32)
        q_pos = start + jnp.arange(SB_BLOCK)
        causal = key_pos[None, :] < q_pos[:, None]
        log_keep = jnp.where(causal, -jax.nn.softplus(z), 0.0)
        later = lax.cumsum(log_keep, axis=3, reverse=True) - log_keep
        w = jnp.where(causal, jnp.exp(jax.nn.log_sigmoid(z) + later), 0.0)
        return jnp.einsum("bhqk,bkhd->bqhd", w, v.astype(f32))

    out = lax.map(block, jnp.arange(seq // SB_BLOCK))
    out = jnp.moveaxis(out, 0, 1).reshape(bsz, seq, GROUP_W)
    return out.astype(q.dtype)


def setup_inputs(seed: int = 0) -> dict:
    key = jax.random.key(seed)
    ks = jax.random.split(key, 34)
    f32 = jnp.float32

    def nrm(k, shape, scale):
        return jax.random.normal(k, shape, f32) * scale

    def gain(k, shape):
        return 1.0 + 0.02 * jax.random.normal(k, shape, f32)

    G, P, C = S5_GROUPS, S5_STATE, S5_GROUP_CH
    n_frac = jnp.arange(GROUP_W, dtype=f32) / (GROUP_W - 1)
    w0_profile = -7.0 + 5.0 * n_frac ** 0.85 + 0.5
    return {
        "x": nrm(ks[0], (BATCH, SEQ, D_MODEL), 1.0),
        "norm_mix_pre": gain(ks[1], (DEPTH, D_MODEL)),
        "norm_mix_post": gain(ks[2], (DEPTH, D_MODEL)),
        "norm_ffn_pre": gain(ks[3], (DEPTH, D_MODEL)),
        "norm_ffn_post": gain(ks[4], (DEPTH, D_MODEL)),
        "w_in_first": nrm(ks[5], (D_MODEL, N_IN0), D_MODEL ** -0.5),
        "w_in_rest": nrm(ks[6], (DEPTH - 1, D_MODEL, N_IN), D_MODEL ** -0.5),
        "w_out": nrm(ks[7], (DEPTH, D_MODEL, D_MODEL), D_MODEL ** -0.5),
        "s5_a_re": -0.5 + nrm(ks[8], (DEPTH, G, P), 0.01),
        "s5_a_im": math.pi * jnp.arange(P, dtype=f32) + nrm(ks[9], (DEPTH, G, P), 0.01),
        "s5_log_dt": jax.random.uniform(ks[10], (DEPTH, G), f32, math.log(1e-3), math.log(1e-1)),
        "s5_b_re": nrm(ks[11], (DEPTH, G, P, C), (2 * C) ** -0.5),
        "s5_b_im": nrm(ks[12], (DEPTH, G, P, C), (2 * C) ** -0.5),
        "s5_c_re": nrm(ks[13], (DEPTH, G, C, P), (2 * P) ** -0.5),
        "s5_c_im": nrm(ks[14], (DEPTH, G, C, P), (2 * P) ** -0.5),
        "s5_d": nrm(ks[15], (DEPTH, GROUP_W), 1.0),
        "s5_glu_w1": nrm(ks[16], (DEPTH, GROUP_W, GROUP_W), GROUP_W ** -0.5),
        "s5_glu_w2": nrm(ks[17], (DEPTH, GROUP_W, GROUP_W), GROUP_W ** -0.5),
        "rw_mu": jax.random.uniform(ks[18], (DEPTH, N_RW0), f32),
        "rw_vres_mu": jax.random.uniform(ks[19], (DEPTH - 1, RWKV_LORA_V), f32),
        "rw_w0": w0_profile + nrm(ks[20], (DEPTH, GROUP_W), 0.1),
        "rw_w2": nrm(ks[21], (DEPTH, RWKV_LORA_W, GROUP_W), 0.5 * RWKV_LORA_W ** -0.5),
        "rw_a0": nrm(ks[22], (DEPTH, GROUP_W), 0.1),
        "rw_a2": nrm(ks[23], (DEPTH, RWKV_LORA_A, GROUP_W), RWKV_LORA_A ** -0.5),
        "rw_g2": nrm(ks[24], (DEPTH, RWKV_LORA_G, GROUP_W), RWKV_LORA_G ** -0.5),
        "rw_v0": nrm(ks[25], (DEPTH - 1, GROUP_W), 0.1),
        "rw_v2": nrm(ks[26], (DEPTH - 1, RWKV_LORA_V, GROUP_W), RWKV_LORA_V ** -0.5),
        "rw_k_k": 0.85 + nrm(ks[27], (DEPTH, GROUP_W), 0.02),
        "rw_k_a": 1.0 + nrm(ks[28], (DEPTH, GROUP_W), 0.02),
        "rw_r_k": nrm(ks[29], (DEPTH, GROUP_W), 0.1),
        "rw_ln_w": gain(ks[30], (DEPTH, GROUP_W)),
        "rw_ln_b": nrm(ks[31], (DEPTH, GROUP_W), 0.02),
        "w_up": nrm(ks[32], (DEPTH, D_MODEL, D_FF), D_MODEL ** -0.5),
        "w_down": nrm(ks[33], (DEPTH, D_FF, D_MODEL), D_FF ** -0.5),
    }


def reference(x, norm_mix_pre, norm_mix_post, norm_ffn_pre, norm_ffn_post,
              w_in_first, w_in_rest, w_out,
              s5_a_re, s5_a_im, s5_log_dt, s5_b_re, s5_b_im, s5_c_re, s5_c_im,
              s5_d, s5_glu_w1, s5_glu_w2,
              rw_mu, rw_vres_mu, rw_w0, rw_w2, rw_a0, rw_a2, rw_g2, rw_v0, rw_v2,
              rw_k_k, rw_k_a, rw_r_k, rw_ln_w, rw_ln_b,
              w_up, w_down):
    seq = x.shape[1]
    cos, sin = rope_tables(seq)
    v_first = None
    for l in range(DEPTH):
        h = rmsnorm(x, norm_mix_pre[l])
        w_in = w_in_first if l == 0 else w_in_rest[l - 1]
        proj = h @ w_in
        s5_u = proj[..., OFF_S5:OFF_RET]
        ret_q, ret_k, ret_v, ret_g = jnp.split(proj[..., OFF_RET:OFF_SB], 4, axis=-1)
        sb_q, sb_k, sb_v = jnp.split(proj[..., OFF_SB:OFF_RW], 3, axis=-1)
        rw_cols = proj[..., OFF_RW:N_IN0]
        out_s5 = s5_mixer(s5_u, s5_a_re[l], s5_a_im[l], s5_log_dt[l], s5_b_re[l], s5_b_im[l],
                          s5_c_re[l], s5_c_im[l], s5_d[l], s5_glu_w1[l], s5_glu_w2[l])
        out_ret = retention_mixer(ret_q, ret_k, ret_v, ret_g, cos, sin)
        if l == 0:
            out_rw, v_first = rwkv7_mixer(
                rw_cols, None, None, rw_mu[l], None, rw_w0[l], rw_w2[l], rw_a0[l], rw_a2[l],
                rw_g2[l], None, None, rw_k_k[l], rw_k_a[l], rw_r_k[l], rw_ln_w[l], rw_ln_b[l])
        else:
            out_rw, v_first = rwkv7_mixer(
                rw_cols, proj[..., N_IN0:], v_first, rw_mu[l], rw_vres_mu[l - 1], rw_w0[l],
                rw_w2[l], rw_a0[l], rw_a2[l], rw_g2[l], rw_v0[l - 1], rw_v2[l - 1],
                rw_k_k[l], rw_k_a[l], rw_r_k[l], rw_ln_w[l], rw_ln_b[l])
        out_sb = stick_breaking_mixer(sb_q, sb_k, sb_v)
        mixed = jnp.concatenate([out_s5, out_ret, out_rw, out_sb], axis=-1) @ w_out[l]
        x = x + rmsnorm(mixed, norm_mix_post[l])
        h = rmsnorm(x, norm_ffn_pre[l])
        f = jnp.square(jax.nn.relu(h @ w_up[l])) @ w_down[l]
        x = x + rmsnorm(f, norm_ffn_post[l])
    return x
```

```python
import functools
import math

import jax
import jax.numpy as jnp
from jax import lax
from jax.experimental import pallas as pl
from jax.experimental.pallas import tpu as pltpu

F32 = jnp.float32
BF16 = jnp.bfloat16

D_MODEL = 1024
DEPTH = 4
GROUP_W = 256
HEAD_DIM = 64
N_HEADS = 4
S5_GROUP_CH = 16
S5_GROUPS = 16
S5_STATE = 64
S5_LANES = S5_GROUPS * S5_STATE
RET_CHUNK = 128
ROPE_BASE = 10000.0
RWKV_LORA_W = 64
RWKV_LORA_A = 64
RWKV_LORA_V = 32
RWKV_LORA_G = 128
RWKV_GN_EPS = 64e-5
RW_CHUNK = 64
SB_BLOCK = 128
D_FF = 4 * D_MODEL
NORM_EPS = 1e-6

OFF_RET = GROUP_W
OFF_SB = OFF_RET + 4 * GROUP_W
OFF_RW = OFF_SB + 3 * GROUP_W
N_IN0 = OFF_RW + 4 * GROUP_W

P32_RET = 4
P32_S5 = 8
P32_VRES = 9

SUBLANES = 8
VMEM_LIMIT = 56 * 1024 * 1024

NN = (((1,), (0,)), ((), ()))
NT = (((1,), (1,)), ((), ()))
TN = (((0,), (0,)), ((), ()))


def _dot(a, b, dims=NN):
    return lax.dot_general(a, b, dims, preferred_element_type=F32)


def _split(x):
    hi = x.astype(BF16)
    lo = (x - hi.astype(F32)).astype(BF16)
    return hi, lo


def _mm3(a, b, dims=NN):
    ah, al = _split(a)
    bh, bl = _split(b)
    return _dot(ah, bh, dims) + _dot(ah, bl, dims) + _dot(al, bh, dims)


def _mm2(a, b_exact, dims=NN):
    ah, al = _split(a)
    return _dot(ah, b_exact, dims) + _dot(al, b_exact, dims)


def _mm2l(a_exact, b, dims=NN):
    bh, bl = _split(b)
    return _dot(a_exact, bh, dims) + _dot(a_exact, bl, dims)


def _mm1(a, b, dims=NN):
    return _dot(a.astype(BF16), b.astype(BF16), dims)


def _softplus(x):
    return jnp.maximum(x, 0.0) + jnp.log(1.0 + jnp.exp(-jnp.abs(x)))


def _sigmoid(x):
    return 1.0 / (1.0 + jnp.exp(-x))


def _rms(x, gain):
    return x * lax.rsqrt(jnp.mean(x * x, axis=-1, keepdims=True) + NORM_EPS) * gain


def _params(*sem):
    return pltpu.CompilerParams(dimension_semantics=sem, vmem_limit_bytes=VMEM_LIMIT)


def _const_spec(shape):
    zeros = (0,) * len(shape)
    return pl.BlockSpec(shape, lambda *_: zeros)


def _in_proj_kernel(x_ref, g_ref, w_ref, o32_ref, o16_ref, *, n32, n16, tn):
    h = _rms(x_ref[...], g_ref[...]).astype(BF16)
    for c0 in range(0, n32, tn):
        o32_ref[:, c0:c0 + tn] = _dot(h, w_ref[:, c0:c0 + tn])
    for c0 in range(0, n16, tn):
        o16_ref[:, c0:c0 + tn] = _dot(h, w_ref[:, n32 + c0:n32 + c0 + tn]).astype(BF16)


def _in_proj(x2, gain, w, n32, n16, tm=512, tn=256):
    t = x2.shape[0]
    kern = functools.partial(_in_proj_kernel, n32=n32, n16=n16, tn=tn)
    return pl.pallas_call(
        kern,
        grid=(t // tm,),
        in_specs=[pl.BlockSpec((tm, D_MODEL), lambda i: (i, 0)),
                  _const_spec((1, D_MODEL)),
                  _const_spec((D_MODEL, n32 + n16))],
        out_specs=[pl.BlockSpec((tm, n32), lambda i: (i, 0)),
                   pl.BlockSpec((tm, n16), lambda i: (i, 0))],
        out_shape=[jax.ShapeDtypeStruct((t, n32), F32),
                   jax.ShapeDtypeStruct((t, n16), BF16)],
        compiler_params=_params("parallel"),
        name="in_proj",
    )(x2, gain, w)


def _post_kernel(m0_ref, m1_ref, m2_ref, m3_ref, x_ref, wo_ref, gmix_ref, gpre_ref,
                 wup_ref, wdn_ref, gpost_ref, o_ref, acc_ref, *, tf):
    mixed = (_mm1(m0_ref[...], wo_ref[0:GROUP_W, :])
             + _mm1(m1_ref[...], wo_ref[GROUP_W:2 * GROUP_W, :])
             + _mm1(m2_ref[...], wo_ref[2 * GROUP_W:3 * GROUP_W, :])
             + _mm1(m3_ref[...], wo_ref[3 * GROUP_W:4 * GROUP_W, :]))
    x1 = x_ref[...] + _rms(mixed, gmix_ref[...])
    h = _rms(x1, gpre_ref[...]).astype(BF16)
    for c0 in range(0, D_FF, tf):
        a = jnp.maximum(_dot(h, wup_ref[:, c0:c0 + tf]), 0.0)
        part = _dot((a * a).astype(BF16), wdn_ref[c0:c0 + tf, :])
        if c0 == 0:
            acc_ref[...] = part
        else:
            acc_ref[...] += part
    o_ref[...] = x1 + _rms(acc_ref[...], gpost_ref[...])


def _post(mix_outs, x2, wo, gmix, gpre, wup, wdn, gpost, tm=512, tf=512):
    t = x2.shape[0]
    row = lambda w: pl.BlockSpec((tm, w), lambda i: (i, 0))
    single = lambda shape: pl.BlockSpec(shape, lambda i: (0, 0), pipeline_mode=pl.Buffered(1))
    return pl.pallas_call(
        functools.partial(_post_kernel, tf=tf),
        grid=(t // tm,),
        in_specs=[row(GROUP_W)] * 4 + [row(D_MODEL), single((D_MODEL, D_MODEL)),
                  _const_spec((1, D_MODEL)), _const_spec((1, D_MODEL)),
                  single((D_MODEL, D_FF)), single((D_FF, D_MODEL)), _const_spec((1, D_MODEL))],
        out_specs=row(D_MODEL),
        out_shape=jax.ShapeDtypeStruct((t, D_MODEL), F32),
        scratch_shapes=[pltpu.VMEM((tm, D_MODEL), F32)],
        compiler_params=_params("parallel"),
        name="out_proj_ffn",
    )(*mix_outs, x2, wo, gmix, gpre, wup, wdn, gpost)


def _s5_kernel(u_ref, wb_ref, wc_ref, pw_ref, d_ref, w1_ref, w2_ref, o_ref,
               st_ref, xr_ref, xi_ref, *, tb):
    @pl.when(pl.program_id(1) == 0)
    def _():
        st_ref[...] = jnp.zeros_like(st_ref)

    u = u_ref[...]
    bu = _mm1(u, wb_ref[...])
    xr_ref[...] = bu[:, :S5_LANES]
    xi_ref[...] = bu[:, S5_LANES:]

    def tile(j, carry):
        cr, ci = carry
        r0 = pl.multiple_of(j * SUBLANES, SUBLANES)
        xr = xr_ref[pl.ds(r0, SUBLANES), :]
        xi = xi_ref[pl.ds(r0, SUBLANES), :]
        for n, shift in enumerate((1, 2, 4)):
            pr = pw_ref[2 * n]
            pi = pw_ref[2 * n + 1]
            sr = pltpu.roll(xr, shift, axis=0)
            si = pltpu.roll(xi, shift, axis=0)
            xr, xi = xr + pr * sr - pi * si, xi + pr * si + pi * sr
        pr = pw_ref[6]
        pi = pw_ref[7]
        xr, xi = xr + pr * cr - pi * ci, xi + pr * ci + pi * cr
        xr_ref[pl.ds(r0, SUBLANES), :] = xr
        xi_ref[pl.ds(r0, SUBLANES), :] = xi
        return xr[SUBLANES - 1:SUBLANES, :], xi[SUBLANES - 1:SUBLANES, :]

    cr, ci = lax.fori_loop(0, tb // SUBLANES, tile, (st_ref[0:1, :], st_ref[1:2, :]))
    st_ref[0:1, :] = cr
    st_ref[1:2, :] = ci

    y = (_mm1(xr_ref[...], wc_ref[0:S5_LANES, :]) + _mm1(xi_ref[...], wc_ref[S5_LANES:, :])
         + d_ref[...] * u)
    y = jax.nn.gelu(y).astype(BF16)
    o_ref[...] = _dot(y, w1_ref[...]) * _sigmoid(_dot(y, w2_ref[...]))


def _s5(p32, bsz, seq, wb, wc, pw, d, w1, w2, tb=512):
    nb = seq // tb
    return pl.pallas_call(
        functools.partial(_s5_kernel, tb=tb),
        grid=(bsz, nb),
        in_specs=[pl.BlockSpec((tb, GROUP_W), lambda b, i: (b * nb + i, P32_S5)),
                  _const_spec((GROUP_W, 2 * S5_LANES)),
                  _const_spec((2 * S5_LANES, GROUP_W)),
                  _const_spec((8, SUBLANES, S5_LANES)),
                  _const_spec((1, GROUP_W)),
                  _const_spec((GROUP_W, GROUP_W)),
                  _const_spec((GROUP_W, GROUP_W))],
        out_specs=pl.BlockSpec((tb, GROUP_W), lambda b, i: (b * nb + i, 0)),
        out_shape=jax.ShapeDtypeStruct((bsz * seq, GROUP_W), F32),
        scratch_shapes=[pltpu.VMEM((SUBLANES, S5_LANES), F32),
                        pltpu.VMEM((tb, S5_LANES), F32),
                        pltpu.VMEM((tb, S5_LANES), F32)],
        compiler_params=_params("parallel", "arbitrary"),
        name="s5",
    )(p32, wb, wc, pw, d, w1, w2)


def _s5_tables(a_re, a_im, log_dt, b_re, b_im, c_re, c_im):
    dt = jnp.exp(log_dt)[:, None]
    mag = jnp.exp(a_re * dt)
    ab_re = mag * jnp.cos(a_im * dt)
    ab_im = mag * jnp.sin(a_im * dt)
    den = a_re * a_re + a_im * a_im
    num_re = ab_re - 1.0
    zoh_re = (num_re * a_re + ab_im * a_im) / den
    zoh_im = (ab_im * a_re - num_re * a_im) / den
    bb_re = zoh_re[..., None] * b_re - zoh_im[..., None] * b_im
    bb_im = zoh_re[..., None] * b_im + zoh_im[..., None] * b_re
    eye = jnp.eye(S5_GROUPS, dtype=F32)
    blk_in = lambda m: jnp.einsum("gpc,gh->gchp", m, eye).reshape(GROUP_W, S5_LANES)
    blk_out = lambda m: jnp.einsum("gcp,gh->gphc", m, eye).reshape(S5_LANES, GROUP_W)
    wb = jnp.concatenate([blk_in(bb_re), blk_in(bb_im)], axis=1).astype(BF16)
    wc = jnp.concatenate([blk_out(c_re), -blk_out(c_im)], axis=0).astype(BF16)

    def power(n):
        m = jnp.exp(n * (a_re * dt)[None])
        ang = n * (a_im * dt)[None]
        return ((m * jnp.cos(ang)).reshape(-1, S5_LANES), (m * jnp.sin(ang)).reshape(-1, S5_LANES))

    rows = jnp.arange(SUBLANES, dtype=F32)[:, None, None]
    tabs = []
    for shift in (1, 2, 4):
        pr, pi = power(jnp.full_like(rows, float(shift)))
        keep = (jnp.arange(SUBLANES) >= shift)[:, None]
        tabs += [jnp.where(keep, pr, 0.0), jnp.where(keep, pi, 0.0)]
    pr, pi = power(rows + 1.0)
    tabs += [pr, pi]
    return wb, wc, jnp.stack(tabs)


def _rope(t, c, s_up, s_dn):
    return (t * c + pltpu.roll(t, HEAD_DIM // 2, axis=1) * s_up
            + pltpu.roll(t, GROUP_W - HEAD_DIM // 2, axis=1) * s_dn)


def _ret_kernel(q_ref, k_ref, v_ref, g_ref, cos_ref, sup_ref, sdn_ref, intra_ref, qd_ref, kd_ref,
                cd_ref, hm_ref, bd_ref, seg_ref, o_ref, st_ref, *, tb):
    @pl.when(pl.program_id(1) == 0)
    def _():
        st_ref[...] = jnp.zeros_like(st_ref)

    for c in range(tb // RET_CHUNK):
        rows = slice(c * RET_CHUNK, (c + 1) * RET_CHUNK)
        cos = cos_ref[rows, :]
        sup = sup_ref[rows, :]
        sdn = sdn_ref[rows, :]
        q = _rope(q_ref[rows, :], cos, sup, sdn)
        k = _rope(k_ref[rows, :], cos, sup, sdn) * HEAD_DIM ** -0.5
        v = v_ref[rows, :]
        kb = k.astype(BF16)
        vb = v.astype(BF16)
        state = st_ref[...]
        o = _mm1(q * qd_ref[...], state)
        for h in range(N_HEADS):
            hm = hm_ref[h:h + 1, :]
            scores = _dot((q * hm).astype(BF16), kb, NT) * intra_ref[h]
            o = o + hm * _dot(scores.astype(BF16), vb)
        st_ref[...] = state * cd_ref[...] + bd_ref[...] * _dot((k * kd_ref[...]).astype(BF16), vb, TN)
        ms = _mm2(o * o, seg_ref[...]) * (1.0 / HEAD_DIM)
        o = o * lax.rsqrt(ms + NORM_EPS)
        g = g_ref[rows, :]
        o_ref[rows, :] = g * _sigmoid(g) * o


def _retention(p32, bsz, seq, tabs, consts, tb=512):
    nb = seq // tb
    col = lambda j: pl.BlockSpec((tb, GROUP_W), lambda b, i: (b * nb + i, j))
    pos = pl.BlockSpec((tb, GROUP_W), lambda b, i: (i, 0))
    cos, sup, sdn, intra, qd, kd, cd = tabs
    return pl.pallas_call(
        functools.partial(_ret_kernel, tb=tb),
        grid=(bsz, nb),
        in_specs=[col(P32_RET), col(P32_RET + 1), col(P32_RET + 2), col(P32_RET + 3), pos, pos, pos,
                  _const_spec((N_HEADS, RET_CHUNK, RET_CHUNK)),
                  _const_spec((RET_CHUNK, GROUP_W)), _const_spec((RET_CHUNK, GROUP_W)),
                  _const_spec((GROUP_W, GROUP_W)), _const_spec((N_HEADS, GROUP_W)),
                  _const_spec((GROUP_W, GROUP_W)), _const_spec((GROUP_W, GROUP_W))],
        out_specs=pl.BlockSpec((tb, GROUP_W), lambda b, i: (b * nb + i, 0)),
        out_shape=jax.ShapeDtypeStruct((bsz * seq, GROUP_W), F32),
        scratch_shapes=[pltpu.VMEM((GROUP_W, GROUP_W), F32)],
        compiler_params=_params("parallel", "arbitrary"),
        name="retention",
    )(p32, p32, p32, p32, cos, sup, sdn, intra, qd, kd, cd, consts["hm"], consts["bd"], consts["seg"])


def _retention_tables(seq):
    inv_freq = ROPE_BASE ** (-jnp.arange(0, HEAD_DIM, 2, dtype=F32) / HEAD_DIM)
    ang = jnp.arange(seq, dtype=F32)[:, None] * inv_freq[None, :]
    cos, sin = jnp.cos(ang), jnp.sin(ang)
    zero = jnp.zeros_like(sin)
    cos_t = jnp.tile(jnp.concatenate([cos, cos], axis=1), (1, N_HEADS))
    sup_t = jnp.tile(jnp.concatenate([zero, sin], axis=1), (1, N_HEADS))
    sdn_t = jnp.tile(jnp.concatenate([-sin, zero], axis=1), (1, N_HEADS))
    log_gamma = jnp.log1p(-jnp.exp2(-5.0 - jnp.arange(N_HEADS, dtype=F32)))
    idx = jnp.arange(RET_CHUNK, dtype=F32)
    rel = idx[:, None] - idx[None, :]
    intra = jnp.where(rel >= 0, jnp.exp(log_gamma[:, None, None] * jnp.maximum(rel, 0.0)), 0.0)
    lanes = jnp.repeat(log_gamma, HEAD_DIM)[None, :]
    qd = jnp.exp(lanes * (idx + 1.0)[:, None])
    kd = jnp.exp(lanes * (RET_CHUNK - 1.0 - idx)[:, None])
    cd = jnp.broadcast_to(jnp.exp(lanes * RET_CHUNK), (GROUP_W, GROUP_W))
    return cos_t, sup_t, sdn_t, intra, qd, kd, cd


def _sb_kernel(q_ref, k_ref, v_ref, hm_ref, cum_ref, o_ref):
    qi = pl.program_id(1)
    q = q_ref[...] * HEAD_DIM ** -0.5
    cum = cum_ref[...]
    row = lax.broadcasted_iota(jnp.int32, (SB_BLOCK, SB_BLOCK), 0)
    colm = lax.broadcasted_iota(jnp.int32, (SB_BLOCK, SB_BLOCK), 1)
    strict = colm < row

    def tile(qh, kb, vb, run, diag):
        z = _dot(qh, kb, NT)
        l1p = jnp.log(1.0 + jnp.exp(-jnp.abs(z)))
        log_keep = -(jnp.maximum(z, 0.0) + l1p)
        log_beta = jnp.minimum(z, 0.0) - l1p
        if diag:
            log_keep = jnp.where(strict, log_keep, 0.0)
        sums = _mm2(log_keep, cum)
        w = jnp.exp(log_beta + sums[:, :SB_BLOCK] + run)
        if diag:
            w = jnp.where(strict, w, 0.0)
        return _dot(w.astype(BF16), vb), run + sums[:, SB_BLOCK:]

    out = jnp.zeros((SB_BLOCK, GROUP_W), F32)
    r0 = pl.multiple_of(qi * SB_BLOCK, SB_BLOCK)
    k_diag = k_ref[pl.ds(r0, SB_BLOCK), :]
    v_diag = v_ref[pl.ds(r0, SB_BLOCK), :]
    for h in range(N_HEADS):
        hm = hm_ref[h:h + 1, :]
        qh = q * hm.astype(BF16)
        acc, run = tile(qh, k_diag, v_diag, jnp.zeros((SB_BLOCK, SB_BLOCK), F32), True)

        def body(jj, carry, qh=qh):
            acc, run = carry
            s0 = pl.multiple_of((qi - 1 - jj) * SB_BLOCK, SB_BLOCK)
            part, run = tile(qh, k_ref[pl.ds(s0, SB_BLOCK), :], v_ref[pl.ds(s0, SB_BLOCK), :], run, False)
            return acc + part, run

        acc, run = lax.fori_loop(0, qi, body, (acc, run))
        out = out + hm * acc
    o_ref[...] = out


def _stick_breaking(p16, bsz, seq, consts):
    nq = seq // SB_BLOCK
    return pl.pallas_call(
        _sb_kernel,
        grid=(bsz, nq),
        in_specs=[pl.BlockSpec((SB_BLOCK, GROUP_W), lambda b, i: (b * nq + i, 0)),
                  pl.BlockSpec((seq, GROUP_W), lambda b, i: (b, 1)),
                  pl.BlockSpec((seq, GROUP_W), lambda b, i: (b, 2)),
                  _const_spec((N_HEADS, GROUP_W)),
                  _const_spec((SB_BLOCK, 2 * SB_BLOCK))],
        out_specs=pl.BlockSpec((SB_BLOCK, GROUP_W), lambda b, i: (b * nq + i, 0)),
        out_shape=jax.ShapeDtypeStruct((bsz * seq, GROUP_W), F32),
        compiler_params=_params("parallel", "arbitrary"),
        name="stick_breaking",
    )(p16, p16, p16, consts["hm"], consts["sb_cum"])


def _tile4(x):
    return jnp.concatenate([x, x, x, x], axis=0)


def _fold4(x):
    n = x.shape[0] // N_HEADS
    return x[0:n] + x[n:2 * n] + x[2 * n:3 * n] + x[3 * n:4 * n]


def _shifted(cur, prev_ref, first_row):
    sh = jnp.where(first_row, prev_ref[0:1, :], pltpu.roll(cur, 1, axis=0))
    prev_ref[0:1, :] = cur[cur.shape[0] - 1:, :]
    return sh


def _rwkv_kernel(*refs, tb, has_vres):
    if has_vres:
        (cols_ref, vres_ref, vfirst_ref, mu_ref, vmu_ref, w0_ref, w2_ref, a0_ref, a2_ref, g2_ref,
         v0_ref, v2_ref, kk_ref, ka_ref, rk_ref, lnw_ref, lnb_ref,
         hm4_ref, sl_ref, il_ref, eye_ref, ltri_ref, seg_ref,
         o_ref, z_ref, prev_ref, prevv_ref, r_s, k_s, v_s, a_s, b_s, lw_s, y_s) = refs
    else:
        (cols_ref, mu_ref, w0_ref, w2_ref, a0_ref, a2_ref, g2_ref,
         kk_ref, ka_ref, rk_ref, lnw_ref, lnb_ref,
         hm4_ref, sl_ref, il_ref, eye_ref, ltri_ref, seg_ref,
         o_ref, vout_ref, z_ref, prev_ref, r_s, k_s, v_s, a_s, b_s, lw_s, y_s) = refs

    @pl.when(pl.program_id(1) == 0)
    def _():
        z_ref[...] = jnp.zeros_like(z_ref)
        prev_ref[...] = jnp.zeros_like(prev_ref)
        if has_vres:
            prevv_ref[...] = jnp.zeros_like(prevv_ref)

    first_row = lax.broadcasted_iota(jnp.int32, (tb, 1), 0) == 0
    cols = cols_ref[...]
    xs = cols + (_shifted(cols, prev_ref, first_row) - cols) * mu_ref[...]
    r = xs[:, 0:GROUP_W]
    k = xs[:, GROUP_W:2 * GROUP_W]
    v = xs[:, 2 * GROUP_W:3 * GROUP_W]
    wa = xs[:, 3 * GROUP_W:3 * GROUP_W + 128]
    gd = xs[:, 3 * GROUP_W + 128:4 * GROUP_W]
    w_log = -_softplus(-(w0_ref[...] + _mm1(jnp.tanh(wa), w2_ref[...]))) - 0.5
    lw = -jnp.exp(w_log)
    a = _sigmoid(a0_ref[...] + _mm1(wa, a2_ref[...]))
    g = _mm1(_sigmoid(gd), g2_ref[...])
    if has_vres:
        vr = vres_ref[...]
        vx = vr + (_shifted(vr, prevv_ref, first_row) - vr) * vmu_ref[...]
        v = v + (vfirst_ref[...] - v) * _sigmoid(v0_ref[...] + _mm1(vx, v2_ref[...]))
    else:
        vout_ref[...] = v
    seg = seg_ref[...]
    kk = k * kk_ref[...]
    kk = kk * lax.rsqrt(jnp.maximum(_mm2(kk * kk, seg), 1e-12))
    k = k * (1.0 + (a - 1.0) * ka_ref[...])
    r_s[...] = r
    k_s[...] = k
    v_s[...] = v
    a_s[...] = -kk
    b_s[...] = kk * a
    lw_s[...] = lw

    hm4 = hm4_ref[...]
    sl = sl_ref[...]
    il = il_ref[...]
    eye = eye_ref[...]

    def chunk(ci, carry):
        rows = pl.ds(pl.multiple_of(ci * RW_CHUNK, RW_CHUNK), RW_CHUNK)
        rc, kc, vc, ac, bc, lwc = r_s[rows, :], k_s[rows, :], v_s[rows, :], a_s[rows, :], b_s[rows, :], lw_s[rows, :]
        c = _mm2l(ltri_ref[...], lwc)
        c_end = c[RW_CHUNK - 1:, :]
        e_neg = jnp.exp(-c)
        e_end = jnp.exp(c_end - c)
        at4 = _tile4(ac * jnp.exp(c - lwc)) * hm4
        rt4 = _tile4(rc * jnp.exp(c)) * hm4
        bs4 = _tile4(bc * e_neg)
        ks4 = _tile4(kc * e_neg)
        be4 = _tile4(bc * e_end) * hm4
        ke4 = _tile4(kc * e_end) * hm4
        v4 = _tile4(vc) * hm4
        n_ab = _mm3(at4, bs4, NT) * sl
        n_ak = _mm3(at4, ks4, NT) * sl
        n_rb = _mm3(rt4, bs4, NT) * il
        n_rk = _mm3(rt4, ks4, NT) * il
        inv = eye + n_ab
        pw = n_ab
        for _ in range(int(math.log2(RW_CHUNK)) - 1):
            pw = _mm3(pw, pw)
            inv = inv + _mm3(pw, inv)
        u0 = _mm3(inv, _mm3(n_ak, v4))
        gm = _mm3(inv, at4)
        y0 = _mm3(n_rb, u0) + _mm3(n_rk, v4)
        hmat = rt4 + _mm3(n_rb, gm)
        p = eye * jnp.exp(c_end) + _mm3(be4, gm, TN)
        qm = _mm3(be4, u0, TN) + _mm3(ke4, v4, TN)
        z = z_ref[...]
        y_s[rows, :] = _fold4(y0 + _mm3(hmat, z))
        z_ref[...] = _mm3(p, z) + qm
        return carry

    lax.fori_loop(0, tb // RW_CHUNK, chunk, 0)

    y = y_s[...]
    inv_n = 1.0 / HEAD_DIM
    mean = _mm2(y, seg) * inv_n
    yc = y - mean
    var = _mm2(yc * yc, seg) * inv_n
    y = yc * lax.rsqrt(var + RWKV_GN_EPS) * lnw_ref[...] + lnb_ref[...]
    r = r_s[...]
    k = k_s[...]
    bonus = _mm2(r * k * rk_ref[...], seg) * v_s[...]
    o_ref[...] = (y + bonus) * g


def _rwkv(p32, vfirst, bsz, seq, lw, consts, has_vres, tb=256):
    nb = seq // tb
    rows = lambda w, j: pl.BlockSpec((tb, w), lambda b, i: (b * nb + i, j))
    vec = _const_spec((1, GROUP_W))
    mat = lambda n: _const_spec((n, GROUP_W))
    big = _const_spec((GROUP_W, GROUP_W))
    cnames = ("hm4", "sl", "il", "eye", "ltri", "seg")
    cspecs = [big, big, big, big, _const_spec((RW_CHUNK, RW_CHUNK)), big]
    cargs = [consts[n] for n in cnames]
    cols_spec = pl.BlockSpec((tb, 4 * GROUP_W), lambda b, i: (b * nb + i, 0))
    out_spec = pl.BlockSpec((tb, GROUP_W), lambda b, i: (b * nb + i, 0))
    out_sds = jax.ShapeDtypeStruct((bsz * seq, GROUP_W), F32)
    scratch = [pltpu.VMEM((GROUP_W, GROUP_W), F32), pltpu.VMEM((SUBLANES, 4 * GROUP_W), F32)]
    chunk_scratch = [pltpu.VMEM((tb, GROUP_W), F32)] * 7
    if has_vres:
        in_specs = ([cols_spec, rows(GROUP_W, P32_VRES), rows(GROUP_W, 0), _const_spec((1, 4 * GROUP_W)), vec,
                     vec, mat(128), vec, mat(128), mat(RWKV_LORA_G), vec, mat(GROUP_W), vec, vec, vec, vec, vec]
                    + cspecs)
        args = [p32, p32, vfirst, lw["mu"], lw["vmu"], lw["w0"], lw["w2"], lw["a0"], lw["a2"],
                lw["g2"], lw["v0"], lw["v2"], lw["k_k"], lw["k_a"], lw["r_k"], lw["ln_w"], lw["ln_b"]] + cargs
        out_specs, out_shape = out_spec, out_sds
        scratch = scratch + [pltpu.VMEM((SUBLANES, GROUP_W), F32)]
    else:
        in_specs = ([cols_spec, _const_spec((1, 4 * GROUP_W)), vec, mat(128), vec, mat(128), mat(RWKV_LORA_G),
                     vec, vec, vec, vec, vec] + cspecs)
        args = [p32, lw["mu"], lw["w0"], lw["w2"], lw["a0"], lw["a2"], lw["g2"],
                lw["k_k"], lw["k_a"], lw["r_k"], lw["ln_w"], lw["ln_b"]] + cargs
        out_specs, out_shape = [out_spec, out_spec], [out_sds, out_sds]
    return pl.pallas_call(
        functools.partial(_rwkv_kernel, tb=tb, has_vres=has_vres),
        grid=(bsz, nb),
        in_specs=in_specs,
        out_specs=out_specs,
        out_shape=out_shape,
        scratch_shapes=scratch + chunk_scratch,
        compiler_params=_params("parallel", "arbitrary"),
        name="rwkv7",
    )(*args)


def _constants():
    lane_head = jnp.arange(GROUP_W) // HEAD_DIM
    hm = (lane_head[None, :] == jnp.arange(N_HEADS)[:, None]).astype(F32)
    bd = (lane_head[:, None] == lane_head[None, :])
    seg = bd.astype(BF16)
    pos = jnp.arange(GROUP_W) % RW_CHUNK
    hm4 = bd.astype(F32)
    sl = (bd & (pos[None, :] < pos[:, None])).astype(F32)
    il = (bd & (pos[None, :] <= pos[:, None])).astype(F32)
    eye = jnp.eye(GROUP_W, dtype=F32)
    t = jnp.arange(RW_CHUNK)
    ltri = (t[None, :] <= t[:, None]).astype(BF16)
    j = jnp.arange(SB_BLOCK)
    suffix = (j[:, None] > j[None, :]).astype(BF16)
    sb_cum = jnp.concatenate([suffix, jnp.ones((SB_BLOCK, SB_BLOCK), BF16)], axis=1)
    return dict(hm=hm, bd=bd.astype(F32), seg=seg, hm4=hm4, sl=sl, il=il, eye=eye, ltri=ltri, sb_cum=sb_cum)


def kernel(x, norm_mix_pre, norm_mix_post, norm_ffn_pre, norm_ffn_post, w_in_first, w_in_rest, w_out,
           s5_a_re, s5_a_im, s5_log_dt, s5_b_re, s5_b_im, s5_c_re, s5_c_im, s5_d, s5_glu_w1, s5_glu_w2,
           rw_mu, rw_vres_mu, rw_w0, rw_w2, rw_a0, rw_a2, rw_g2, rw_v0, rw_v2,
           rw_k_k, rw_k_a, rw_r_k, rw_ln_w, rw_ln_b, w_up, w_down):
    bsz, seq, _ = x.shape
    consts = _constants()
    ret_tabs = _retention_tables(seq)
    x2 = x.reshape(bsz * seq, D_MODEL)
    row = lambda vec: vec.reshape(1, -1)
    zpad = lambda m, n: jnp.pad(m, ((0, n - m.shape[0]), (0, 0)))
    v_first = None
    for l in range(DEPTH):
        w_in = w_in_first if l == 0 else w_in_rest[l - 1]
        parts = [w_in[:, OFF_RW:N_IN0], w_in[:, OFF_RET:OFF_SB], w_in[:, :OFF_RET]]
        if l > 0:
            parts.append(jnp.pad(w_in[:, N_IN0:], ((0, 0), (0, GROUP_W - RWKV_LORA_V))))
        n32 = sum(p.shape[1] for p in parts)
        w_cat = jnp.concatenate(parts + [w_in[:, OFF_SB:OFF_RW]], axis=1).astype(BF16)
        p32, p16 = _in_proj(x2, row(norm_mix_pre[l]), w_cat, n32, 3 * GROUP_W)

        wb, wc, pw = _s5_tables(s5_a_re[l], s5_a_im[l], s5_log_dt[l], s5_b_re[l], s5_b_im[l],
                                s5_c_re[l], s5_c_im[l])
        out_s5 = _s5(p32, bsz, seq, wb, wc, pw, row(s5_d[l]),
                     s5_glu_w1[l].astype(BF16), s5_glu_w2[l].astype(BF16))
        out_ret = _retention(p32, bsz, seq, ret_tabs, consts)
        out_sb = _stick_breaking(p16, bsz, seq, consts)

        lw = dict(
            mu=row(rw_mu[l]), w0=row(rw_w0[l]), a0=row(rw_a0[l]),
            w2=zpad(rw_w2[l], 128).astype(BF16),
            a2=jnp.pad(rw_a2[l], ((RWKV_LORA_W, 0), (0, 0))).astype(BF16),
            g2=rw_g2[l].astype(BF16),
            k_k=row(rw_k_k[l]), k_a=row(rw_k_a[l]), r_k=row(rw_r_k[l]),
            ln_w=row(rw_ln_w[l]), ln_b=row(rw_ln_b[l]))
        if l == 0:
            out_rw, v_first = _rwkv(p32, None, bsz, seq, lw, consts, False)
        else:
            lw.update(vmu=row(jnp.pad(rw_vres_mu[l - 1], (0, GROUP_W - RWKV_LORA_V))),
                      v0=row(rw_v0[l - 1]), v2=zpad(rw_v2[l - 1], GROUP_W).astype(BF16))
            out_rw = _rwkv(p32, v_first, bsz, seq, lw, consts, True)

        x2 = _post((out_s5, out_ret, out_rw, out_sb), x2, w_out[l].astype(BF16), row(norm_mix_post[l]),
                   row(norm_ffn_pre[l]), w_up[l].astype(BF16), w_down[l].astype(BF16), row(norm_ffn_post[l]))
    return x2.reshape(bsz, seq, D_MODEL)
```

```python
import functools
import math

import jax
import jax.numpy as jnp
from jax import lax
from jax.experimental import pallas as pl
from jax.experimental.pallas import tpu as pltpu

F32 = jnp.float32
BF16 = jnp.bfloat16

D_MODEL = 1024
DEPTH = 4
GROUP_W = 256
HEAD_DIM = 64
N_HEADS = 4
S5_GROUP_CH = 16
S5_GROUPS = 16
S5_STATE = 64
S5_LANES = S5_GROUPS * S5_STATE
RET_CHUNK = 128
ROPE_BASE = 10000.0
RWKV_LORA_W = 64
RWKV_LORA_A = 64
RWKV_LORA_V = 32
RWKV_LORA_G = 128
RWKV_GN_EPS = 64e-5
RW_CHUNK = 64
SB_BLOCK = 128
SB_Q = 512
D_FF = 4 * D_MODEL
NORM_EPS = 1e-6

OFF_RET = GROUP_W
OFF_SB = OFF_RET + 4 * GROUP_W
OFF_RW = OFF_SB + 3 * GROUP_W
N_IN0 = OFF_RW + 4 * GROUP_W

P32_RET = 4
P32_S5 = 8
P32_VRES = 9

SUBLANES = 8
VMEM_LIMIT = 56 * 1024 * 1024

NN = (((1,), (0,)), ((), ()))
NT = (((1,), (1,)), ((), ()))
TN = (((0,), (0,)), ((), ()))


def _dot(a, b, dims=NN):
    return lax.dot_general(a, b, dims, preferred_element_type=F32)


def _split(x):
    hi = x.astype(BF16)
    lo = (x - hi.astype(F32)).astype(BF16)
    return hi, lo


def _mm2(a, b_exact, dims=NN):
    ah, al = _split(a)
    return _dot(ah, b_exact, dims) + _dot(al, b_exact, dims)


def _mm2l(a_exact, b, dims=NN):
    bh, bl = _split(b)
    return _dot(a_exact, bh, dims) + _dot(a_exact, bl, dims)


def _mm1(a, b, dims=NN):
    return _dot(a.astype(BF16), b.astype(BF16), dims)


def _softplus(x):
    return jnp.maximum(x, 0.0) + jnp.log(1.0 + jnp.exp(-jnp.abs(x)))


def _sigmoid(x):
    return 1.0 / (1.0 + jnp.exp(-x))


def _rms(x, gain):
    return x * lax.rsqrt(jnp.mean(x * x, axis=-1, keepdims=True) + NORM_EPS) * gain


def _params(*sem):
    return pltpu.CompilerParams(dimension_semantics=sem, vmem_limit_bytes=VMEM_LIMIT)


def _const_spec(shape):
    zeros = (0,) * len(shape)
    return pl.BlockSpec(shape, lambda *_: zeros)


def _in_proj_kernel(x_ref, g_ref, w_ref, o32_ref, o16_ref, *, n32, n16, tn):
    h = _rms(x_ref[...], g_ref[...]).astype(BF16)
    for c0 in range(0, n32, tn):
        o32_ref[:, c0:c0 + tn] = _dot(h, w_ref[:, c0:c0 + tn])
    for c0 in range(0, n16, tn):
        o16_ref[:, c0:c0 + tn] = _dot(h, w_ref[:, n32 + c0:n32 + c0 + tn]).astype(BF16)


def _in_proj(x2, gain, w, n32, n16, tm=512, tn=256):
    t = x2.shape[0]
    kern = functools.partial(_in_proj_kernel, n32=n32, n16=n16, tn=tn)
    return pl.pallas_call(
        kern,
        grid=(t // tm,),
        in_specs=[pl.BlockSpec((tm, D_MODEL), lambda i: (i, 0)),
                  _const_spec((1, D_MODEL)),
                  _const_spec((D_MODEL, n32 + n16))],
        out_specs=[pl.BlockSpec((tm, n32), lambda i: (i, 0)),
                   pl.BlockSpec((tm, n16), lambda i: (i, 0))],
        out_shape=[jax.ShapeDtypeStruct((t, n32), F32),
                   jax.ShapeDtypeStruct((t, n16), BF16)],
        compiler_params=_params("parallel"),
        name="in_proj",
    )(x2, gain, w)


def _post_kernel(m0_ref, m1_ref, m2_ref, m3_ref, x_ref, wo_ref, gmix_ref, gpre_ref,
                 wup_ref, wdn_ref, gpost_ref, o_ref, acc_ref, *, tf):
    mixed = (_mm1(m0_ref[...], wo_ref[0:GROUP_W, :])
             + _mm1(m1_ref[...], wo_ref[GROUP_W:2 * GROUP_W, :])
             + _mm1(m2_ref[...], wo_ref[2 * GROUP_W:3 * GROUP_W, :])
             + _mm1(m3_ref[...], wo_ref[3 * GROUP_W:4 * GROUP_W, :]))
    x1 = x_ref[...] + _rms(mixed, gmix_ref[...])
    h = _rms(x1, gpre_ref[...]).astype(BF16)
    for c0 in range(0, D_FF, tf):
        a = jnp.maximum(_dot(h, wup_ref[:, c0:c0 + tf]), 0.0)
        part = _dot((a * a).astype(BF16), wdn_ref[c0:c0 + tf, :])
        if c0 == 0:
            acc_ref[...] = part
        else:
            acc_ref[...] += part
    o_ref[...] = x1 + _rms(acc_ref[...], gpost_ref[...])


def _post(mix_outs, x2, wo, gmix, gpre, wup, wdn, gpost, tm=512, tf=512):
    t = x2.shape[0]
    row = lambda w: pl.BlockSpec((tm, w), lambda i: (i, 0))
    single = lambda shape: pl.BlockSpec(shape, lambda i: (0, 0), pipeline_mode=pl.Buffered(1))
    return pl.pallas_call(
        functools.partial(_post_kernel, tf=tf),
        grid=(t // tm,),
        in_specs=[row(GROUP_W)] * 4 + [row(D_MODEL), single((D_MODEL, D_MODEL)),
                  _const_spec((1, D_MODEL)), _const_spec((1, D_MODEL)),
                  single((D_MODEL, D_FF)), single((D_FF, D_MODEL)), _const_spec((1, D_MODEL))],
        out_specs=row(D_MODEL),
        out_shape=jax.ShapeDtypeStruct((t, D_MODEL), F32),
        scratch_shapes=[pltpu.VMEM((tm, D_MODEL), F32)],
        compiler_params=_params("parallel"),
        name="out_proj_ffn",
    )(*mix_outs, x2, wo, gmix, gpre, wup, wdn, gpost)


def _s5_kernel(u_ref, wb_ref, wc_ref, pw_ref, d_ref, w1_ref, w2_ref, o_ref,
               st_ref, xr_ref, xi_ref, *, tb):
    @pl.when(pl.program_id(1) == 0)
    def _():
        st_ref[...] = jnp.zeros_like(st_ref)

    u = u_ref[...]
    bu = _mm1(u, wb_ref[...])
    xr_ref[...] = bu[:, :S5_LANES]
    xi_ref[...] = bu[:, S5_LANES:]

    def tile(j, carry):
        cr, ci = carry
        r0 = pl.multiple_of(j * SUBLANES, SUBLANES)
        xr = xr_ref[pl.ds(r0, SUBLANES), :]
        xi = xi_ref[pl.ds(r0, SUBLANES), :]
        for n, shift in enumerate((1, 2, 4)):
            pr = pw_ref[2 * n]
            pi = pw_ref[2 * n + 1]
            sr = pltpu.roll(xr, shift, axis=0)
            si = pltpu.roll(xi, shift, axis=0)
            xr, xi = xr + pr * sr - pi * si, xi + pr * si + pi * sr
        pr = pw_ref[6]
        pi = pw_ref[7]
        xr, xi = xr + pr * cr - pi * ci, xi + pr * ci + pi * cr
        xr_ref[pl.ds(r0, SUBLANES), :] = xr
        xi_ref[pl.ds(r0, SUBLANES), :] = xi
        return xr[SUBLANES - 1:SUBLANES, :], xi[SUBLANES - 1:SUBLANES, :]

    cr, ci = lax.fori_loop(0, tb // SUBLANES, tile, (st_ref[0:1, :], st_ref[1:2, :]))
    st_ref[0:1, :] = cr
    st_ref[1:2, :] = ci

    y = (_mm1(xr_ref[...], wc_ref[0:S5_LANES, :]) + _mm1(xi_ref[...], wc_ref[S5_LANES:, :])
         + d_ref[...] * u)
    y = jax.nn.gelu(y).astype(BF16)
    o_ref[...] = _dot(y, w1_ref[...]) * _sigmoid(_dot(y, w2_ref[...]))


def _s5(p32, bsz, seq, wb, wc, pw, d, w1, w2, tb=512):
    nb = seq // tb
    return pl.pallas_call(
        functools.partial(_s5_kernel, tb=tb),
        grid=(bsz, nb),
        in_specs=[pl.BlockSpec((tb, GROUP_W), lambda b, i: (b * nb + i, P32_S5)),
                  _const_spec((GROUP_W, 2 * S5_LANES)),
                  _const_spec((2 * S5_LANES, GROUP_W)),
                  _const_spec((8, SUBLANES, S5_LANES)),
                  _const_spec((1, GROUP_W)),
                  _const_spec((GROUP_W, GROUP_W)),
                  _const_spec((GROUP_W, GROUP_W))],
        out_specs=pl.BlockSpec((tb, GROUP_W), lambda b, i: (b * nb + i, 0)),
        out_shape=jax.ShapeDtypeStruct((bsz * seq, GROUP_W), F32),
        scratch_shapes=[pltpu.VMEM((SUBLANES, S5_LANES), F32),
                        pltpu.VMEM((tb, S5_LANES), F32),
                        pltpu.VMEM((tb, S5_LANES), F32)],
        compiler_params=_params("parallel", "arbitrary"),
        name="s5",
    )(p32, wb, wc, pw, d, w1, w2)


def _s5_tables(a_re, a_im, log_dt, b_re, b_im, c_re, c_im):
    dt = jnp.exp(log_dt)[:, None]
    mag = jnp.exp(a_re * dt)
    ab_re = mag * jnp.cos(a_im * dt)
    ab_im = mag * jnp.sin(a_im * dt)
    den = a_re * a_re + a_im * a_im
    num_re = ab_re - 1.0
    zoh_re = (num_re * a_re + ab_im * a_im) / den
    zoh_im = (ab_im * a_re - num_re * a_im) / den
    bb_re = zoh_re[..., None] * b_re - zoh_im[..., None] * b_im
    bb_im = zoh_re[..., None] * b_im + zoh_im[..., None] * b_re
    eye = jnp.eye(S5_GROUPS, dtype=F32)
    blk_in = lambda m: jnp.einsum("gpc,gh->gchp", m, eye).reshape(GROUP_W, S5_LANES)
    blk_out = lambda m: jnp.einsum("gcp,gh->gphc", m, eye).reshape(S5_LANES, GROUP_W)
    wb = jnp.concatenate([blk_in(bb_re), blk_in(bb_im)], axis=1).astype(BF16)
    wc = jnp.concatenate([blk_out(c_re), -blk_out(c_im)], axis=0).astype(BF16)

    def power(n):
        m = jnp.exp(n * (a_re * dt)[None])
        ang = n * (a_im * dt)[None]
        return ((m * jnp.cos(ang)).reshape(-1, S5_LANES), (m * jnp.sin(ang)).reshape(-1, S5_LANES))

    rows = jnp.arange(SUBLANES, dtype=F32)[:, None, None]
    tabs = []
    for shift in (1, 2, 4):
        pr, pi = power(jnp.full_like(rows, float(shift)))
        keep = (jnp.arange(SUBLANES) >= shift)[:, None]
        tabs += [jnp.where(keep, pr, 0.0), jnp.where(keep, pi, 0.0)]
    pr, pi = power(rows + 1.0)
    tabs += [pr, pi]
    return wb, wc, jnp.stack(tabs)


def _rope(t, c, s_up, s_dn):
    return (t * c + pltpu.roll(t, HEAD_DIM // 2, axis=1) * s_up
            + pltpu.roll(t, GROUP_W - HEAD_DIM // 2, axis=1) * s_dn)


def _ret_kernel(q_ref, k_ref, v_ref, g_ref, cos_ref, sup_ref, sdn_ref, intra_ref, qd_ref, kd_ref,
                cd_ref, hm_ref, bd_ref, seg_ref, o_ref, st_ref, *, tb):
    @pl.when(pl.program_id(1) == 0)
    def _():
        st_ref[...] = jnp.zeros_like(st_ref)

    for c in range(tb // RET_CHUNK):
        rows = slice(c * RET_CHUNK, (c + 1) * RET_CHUNK)
        cos = cos_ref[rows, :]
        sup = sup_ref[rows, :]
        sdn = sdn_ref[rows, :]
        q = _rope(q_ref[rows, :], cos, sup, sdn)
        k = _rope(k_ref[rows, :], cos, sup, sdn) * HEAD_DIM ** -0.5
        v = v_ref[rows, :]
        kb = k.astype(BF16)
        vb = v.astype(BF16)
        state = st_ref[...]
        o = _mm1(q * qd_ref[...], state)
        for h in range(N_HEADS):
            hm = hm_ref[h:h + 1, :]
            scores = _dot((q * hm).astype(BF16), kb, NT) * intra_ref[h]
            o = o + hm * _dot(scores.astype(BF16), vb)
        st_ref[...] = state * cd_ref[...] + bd_ref[...] * _dot((k * kd_ref[...]).astype(BF16), vb, TN)
        ms = _mm2(o * o, seg_ref[...]) * (1.0 / HEAD_DIM)
        o = o * lax.rsqrt(ms + NORM_EPS)
        g = g_ref[rows, :]
        o_ref[rows, :] = g * _sigmoid(g) * o


def _retention(p32, bsz, seq, tabs, consts, tb=512):
    nb = seq // tb
    col = lambda j: pl.BlockSpec((tb, GROUP_W), lambda b, i: (b * nb + i, j))
    pos = pl.BlockSpec((tb, GROUP_W), lambda b, i: (i, 0))
    cos, sup, sdn, intra, qd, kd, cd = tabs
    return pl.pallas_call(
        functools.partial(_ret_kernel, tb=tb),
        grid=(bsz, nb),
        in_specs=[col(P32_RET), col(P32_RET + 1), col(P32_RET + 2), col(P32_RET + 3), pos, pos, pos,
                  _const_spec((N_HEADS, RET_CHUNK, RET_CHUNK)),
                  _const_spec((RET_CHUNK, GROUP_W)), _const_spec((RET_CHUNK, GROUP_W)),
                  _const_spec((GROUP_W, GROUP_W)), _const_spec((N_HEADS, GROUP_W)),
                  _const_spec((GROUP_W, GROUP_W)), _const_spec((GROUP_W, GROUP_W))],
        out_specs=pl.BlockSpec((tb, GROUP_W), lambda b, i: (b * nb + i, 0)),
        out_shape=jax.ShapeDtypeStruct((bsz * seq, GROUP_W), F32),
        scratch_shapes=[pltpu.VMEM((GROUP_W, GROUP_W), F32)],
        compiler_params=_params("parallel", "arbitrary"),
        name="retention",
    )(p32, p32, p32, p32, cos, sup, sdn, intra, qd, kd, cd, consts["hm"], consts["bd"], consts["seg"])


def _retention_tables(seq):
    inv_freq = ROPE_BASE ** (-jnp.arange(0, HEAD_DIM, 2, dtype=F32) / HEAD_DIM)
    ang = jnp.arange(seq, dtype=F32)[:, None] * inv_freq[None, :]
    cos, sin = jnp.cos(ang), jnp.sin(ang)
    zero = jnp.zeros_like(sin)
    cos_t = jnp.tile(jnp.concatenate([cos, cos], axis=1), (1, N_HEADS))
    sup_t = jnp.tile(jnp.concatenate([zero, sin], axis=1), (1, N_HEADS))
    sdn_t = jnp.tile(jnp.concatenate([-sin, zero], axis=1), (1, N_HEADS))
    log_gamma = jnp.log1p(-jnp.exp2(-5.0 - jnp.arange(N_HEADS, dtype=F32)))
    idx = jnp.arange(RET_CHUNK, dtype=F32)
    rel = idx[:, None] - idx[None, :]
    intra = jnp.where(rel >= 0, jnp.exp(log_gamma[:, None, None] * jnp.maximum(rel, 0.0)), 0.0)
    lanes = jnp.repeat(log_gamma, HEAD_DIM)[None, :]
    qd = jnp.exp(lanes * (idx + 1.0)[:, None])
    kd = jnp.exp(lanes * (RET_CHUNK - 1.0 - idx)[:, None])
    cd = jnp.broadcast_to(jnp.exp(lanes * RET_CHUNK), (GROUP_W, GROUP_W))
    return cos_t, sup_t, sdn_t, intra, qd, kd, cd


def _sb_kernel(q_ref, k_ref, vt_ref, hm4_ref, cumt_ref, neg1_ref, o_ref, acc_ref, zs_ref):
    qb = pl.program_id(1)
    nsub = SB_Q // SB_BLOCK
    width = N_HEADS * SB_Q
    q = q_ref[...] * HEAD_DIM ** -0.5
    q4 = _tile4(q) * hm4_ref[...]
    cumt = cumt_ref[...]
    neg1 = neg1_ref[...]
    key = lax.broadcasted_iota(jnp.int32, (SB_BLOCK, width), 0)
    qry = lax.broadcasted_iota(jnp.int32, (SB_BLOCK, width), 1) & (SB_Q - 1)
    acc_ref[...] = jnp.zeros_like(acc_ref)

    def logits(blk):
        s0 = pl.multiple_of(blk * SB_BLOCK, SB_BLOCK)
        return _dot(k_ref[pl.ds(s0, SB_BLOCK), :], q4, NT).astype(BF16)

    def finish(zts, blks, run, diags):
        sps, lbs, masks = [], [], []
        for zt, diag in zip(zts, diags):
            spb = jnp.maximum(zt, 0.0) + jnp.log(1.0 + jnp.exp(-jnp.abs(zt)))
            lbs.append((zt - spb).astype(F32))
            masks.append(None if diag is None else key + diag * SB_BLOCK < qry)
            sps.append(spb if diag is None else jnp.where(masks[-1], spb, 0.0))
        laters = []
        for spb in sps:
            laters.append(_dot(cumt, spb) + run)
            run = run + _dot(neg1, spb)[0:1, :]
        for blk, lb, later, mask in zip(blks, lbs, laters, masks):
            w = jnp.exp(lb + later)
            wb = (w if mask is None else jnp.where(mask, w, 0.0)).astype(BF16)
            vt = vt_ref[blk]
            for h in range(N_HEADS):
                acc_ref[h * HEAD_DIM:(h + 1) * HEAD_DIM, :] += _dot(
                    vt[h * HEAD_DIM:(h + 1) * HEAD_DIM, :], wb[:, h * SB_Q:(h + 1) * SB_Q])
        return run

    run = jnp.zeros((1, width), F32)
    top = qb * nsub
    for m in range(nsub - 1, 0, -2):
        blks = (top + m, top + m - 1)
        run = finish([logits(b) for b in blks], blks, run, (m, m - 1))

    @pl.when(qb > 0)
    def _():
        zs_ref[0] = logits(top - 1)
        zs_ref[1] = logits(top - 2)

    def pair(jj, run):
        blk = top - 1 - 2 * jj
        zts = [zs_ref[0], zs_ref[1]]
        zs_ref[0] = logits(jnp.maximum(blk - 2, 0))
        zs_ref[1] = logits(jnp.maximum(blk - 3, 0))
        return finish(zts, (blk, blk - 1), run, (None, None))

    lax.fori_loop(0, qb * (nsub // 2), pair, run)
    o_ref[...] = acc_ref[...].T


def _stick_breaking(p16, bsz, seq, consts):
    nq = seq // SB_Q
    nk = seq // SB_BLOCK
    vt = p16[:, 2 * GROUP_W:].reshape(bsz * nk, SB_BLOCK, GROUP_W).swapaxes(1, 2)
    return pl.pallas_call(
        _sb_kernel,
        grid=(bsz, nq),
        in_specs=[pl.BlockSpec((SB_Q, GROUP_W), lambda b, i: (b * nq + i, 0)),
                  pl.BlockSpec((seq, GROUP_W), lambda b, i: (b, 1)),
                  pl.BlockSpec((nk, GROUP_W, SB_BLOCK), lambda b, i: (b, 0, 0)),
                  _const_spec((N_HEADS * SB_Q, GROUP_W)),
                  _const_spec((SB_BLOCK, SB_BLOCK)),
                  _const_spec((SUBLANES, SB_BLOCK))],
        out_specs=pl.BlockSpec((SB_Q, GROUP_W), lambda b, i: (b * nq + i, 0)),
        out_shape=jax.ShapeDtypeStruct((bsz * seq, GROUP_W), F32),
        scratch_shapes=[pltpu.VMEM((GROUP_W, SB_Q), F32),
                        pltpu.VMEM((2, SB_BLOCK, N_HEADS * SB_Q), BF16)],
        compiler_params=_params("parallel", "arbitrary"),
        name="stick_breaking",
    )(p16, p16, vt, consts["sb_hm4"], consts["sb_cumt"], consts["sb_neg1"])


def _tile4(x):
    return jnp.concatenate([x, x, x, x], axis=0)


def _fold4(x):
    n = x.shape[0] // N_HEADS
    return x[0:n] + x[n:2 * n] + x[2 * n:3 * n] + x[3 * n:4 * n]


def _shifted(cur, prev_ref, first_row):
    sh = jnp.where(first_row, prev_ref[0:1, :], pltpu.roll(cur, 1, axis=0))
    prev_ref[0:1, :] = cur[cur.shape[0] - 1:, :]
    return sh


def _rwkv_kernel(*refs, tb, has_vres):
    if has_vres:
        (cols_ref, vres_ref, vfirst_ref, mu_ref, vmu_ref, w0_ref, w2_ref, a0_ref, a2_ref, g2_ref,
         v0_ref, v2_ref, kk_ref, ka_ref, rk_ref, lnw_ref, lnb_ref,
         hm4_ref, sl_ref, il_ref, eye_ref, ltri_ref, seg_ref,
         o_ref, z_ref, prev_ref, prevv_ref, r_s, k_s, v_s, a_s, b_s, lw_s, y_s) = refs
    else:
        (cols_ref, mu_ref, w0_ref, w2_ref, a0_ref, a2_ref, g2_ref,
         kk_ref, ka_ref, rk_ref, lnw_ref, lnb_ref,
         hm4_ref, sl_ref, il_ref, eye_ref, ltri_ref, seg_ref,
         o_ref, vout_ref, z_ref, prev_ref, r_s, k_s, v_s, a_s, b_s, lw_s, y_s) = refs

    @pl.when(pl.program_id(1) == 0)
    def _():
        z_ref[...] = jnp.zeros_like(z_ref)
        prev_ref[...] = jnp.zeros_like(prev_ref)
        if has_vres:
            prevv_ref[...] = jnp.zeros_like(prevv_ref)

    first_row = lax.broadcasted_iota(jnp.int32, (tb, 1), 0) == 0
    cols = cols_ref[...]
    xs = cols + (_shifted(cols, prev_ref, first_row) - cols) * mu_ref[...]
    r = xs[:, 0:GROUP_W]
    k = xs[:, GROUP_W:2 * GROUP_W]
    v = xs[:, 2 * GROUP_W:3 * GROUP_W]
    wa = xs[:, 3 * GROUP_W:3 * GROUP_W + 128]
    gd = xs[:, 3 * GROUP_W + 128:4 * GROUP_W]
    w_log = -_softplus(-(w0_ref[...] + _mm1(jnp.tanh(wa), w2_ref[...]))) - 0.5
    lw = -jnp.exp(w_log)
    a = _sigmoid(a0_ref[...] + _mm1(wa, a2_ref[...]))
    g = _mm1(_sigmoid(gd), g2_ref[...])
    if has_vres:
        vr = vres_ref[...]
        vx = vr + (_shifted(vr, prevv_ref, first_row) - vr) * vmu_ref[...]
        v = v + (vfirst_ref[...] - v) * _sigmoid(v0_ref[...] + _mm1(vx, v2_ref[...]))
    else:
        vout_ref[...] = v
    seg = seg_ref[...]
    kk = k * kk_ref[...]
    kk = kk * lax.rsqrt(jnp.maximum(_mm2(kk * kk, seg), 1e-12))
    k = k * (1.0 + (a - 1.0) * ka_ref[...])
    r_s[...] = r
    k_s[...] = k
    v_s[...] = v
    a_s[...] = -kk
    b_s[...] = kk * a
    lw_s[...] = lw

    hm4 = hm4_ref[...]
    sl = sl_ref[...]
    il = il_ref[...]
    eye = eye_ref[...]

    w256 = GROUP_W
    for ci in range(tb // RW_CHUNK):
        rows = slice(ci * RW_CHUNK, (ci + 1) * RW_CHUNK)
        rc, kc, vc, ac, bc, lwc = r_s[rows, :], k_s[rows, :], v_s[rows, :], a_s[rows, :], b_s[rows, :], lw_s[rows, :]
        c = _mm2l(ltri_ref[...], lwc)
        c_end = c[RW_CHUNK - 1:, :]
        e_neg = jnp.exp(-c)
        e_end = jnp.exp(c_end - c)
        at4 = _tile4(ac * jnp.exp(c - lwc)) * hm4
        rt4 = _tile4(rc * jnp.exp(c)) * hm4
        at4b = at4.astype(BF16)
        be4b = (_tile4(bc * e_end) * hm4).astype(BF16)
        ke4b = (_tile4(kc * e_end) * hm4).astype(BF16)
        v4b = (_tile4(vc) * hm4).astype(BF16)
        lhs = jnp.concatenate([at4b, rt4.astype(BF16)], axis=0)
        rhs = jnp.concatenate([_tile4((bc * e_neg).astype(BF16)), _tile4((kc * e_neg).astype(BF16))], axis=0)
        nn = _dot(lhs, rhs, NT)
        n_ab = nn[:w256, :w256] * sl
        n_ak = (nn[:w256, w256:] * sl).astype(BF16)
        n_rb = (nn[w256:, :w256] * il).astype(BF16)
        n_rk = (nn[w256:, w256:] * il).astype(BF16)
        inv = eye + n_ab
        pw = n_ab.astype(BF16)
        for _ in range(int(math.log2(RW_CHUNK)) - 1):
            pw = _dot(pw, pw).astype(BF16)
            inv = inv + _dot(pw, inv.astype(BF16))
        ug = _dot(inv.astype(BF16), jnp.concatenate([_dot(n_ak, v4b).astype(BF16), at4b], axis=1))
        ugb = ug.astype(BF16)
        yh = _dot(n_rb, ugb)
        y0 = yh[:, :w256] + _dot(n_rk, v4b)
        hmat = rt4 + yh[:, w256:]
        qp = _dot(be4b, ugb, TN)
        qm = qp[:, :w256] + _dot(ke4b, v4b, TN)
        p = eye * jnp.exp(c_end) + qp[:, w256:]
        yz = _dot(jnp.concatenate([hmat.astype(BF16), p.astype(BF16)], axis=0), z_ref[...].astype(BF16))
        y_s[rows, :] = _fold4(y0 + yz[:w256])
        z_ref[...] = yz[w256:] + qm

    y = y_s[...]
    inv_n = 1.0 / HEAD_DIM
    mean = _mm2(y, seg) * inv_n
    yc = y - mean
    var = _mm2(yc * yc, seg) * inv_n
    y = yc * lax.rsqrt(var + RWKV_GN_EPS) * lnw_ref[...] + lnb_ref[...]
    r = r_s[...]
    k = k_s[...]
    bonus = _mm2(r * k * rk_ref[...], seg) * v_s[...]
    o_ref[...] = (y + bonus) * g


def _rwkv(p32, vfirst, bsz, seq, lw, consts, has_vres, tb=256):
    nb = seq // tb
    rows = lambda w, j: pl.BlockSpec((tb, w), lambda b, i: (b * nb + i, j))
    vec = _const_spec((1, GROUP_W))
    mat = lambda n: _const_spec((n, GROUP_W))
    big = _const_spec((GROUP_W, GROUP_W))
    cnames = ("hm4", "sl", "il", "eye", "ltri", "seg")
    cspecs = [big, big, big, big, _const_spec((RW_CHUNK, RW_CHUNK)), big]
    cargs = [consts[n] for n in cnames]
    cols_spec = pl.BlockSpec((tb, 4 * GROUP_W), lambda b, i: (b * nb + i, 0))
    out_spec = pl.BlockSpec((tb, GROUP_W), lambda b, i: (b * nb + i, 0))
    out_sds = jax.ShapeDtypeStruct((bsz * seq, GROUP_W), F32)
    scratch = [pltpu.VMEM((GROUP_W, GROUP_W), F32), pltpu.VMEM((SUBLANES, 4 * GROUP_W), F32)]
    chunk_scratch = [pltpu.VMEM((tb, GROUP_W), F32)] * 7
    if has_vres:
        in_specs = ([cols_spec, rows(GROUP_W, P32_VRES), rows(GROUP_W, 0), _const_spec((1, 4 * GROUP_W)), vec,
                     vec, mat(128), vec, mat(128), mat(RWKV_LORA_G), vec, mat(GROUP_W), vec, vec, vec, vec, vec]
                    + cspecs)
        args = [p32, p32, vfirst, lw["mu"], lw["vmu"], lw["w0"], lw["w2"], lw["a0"], lw["a2"],
                lw["g2"], lw["v0"], lw["v2"], lw["k_k"], lw["k_a"], lw["r_k"], lw["ln_w"], lw["ln_b"]] + cargs
        out_specs, out_shape = out_spec, out_sds
        scratch = scratch + [pltpu.VMEM((SUBLANES, GROUP_W), F32)]
    else:
        in_specs = ([cols_spec, _const_spec((1, 4 * GROUP_W)), vec, mat(128), vec, mat(128), mat(RWKV_LORA_G),
                     vec, vec, vec, vec, vec] + cspecs)
        args = [p32, lw["mu"], lw["w0"], lw["w2"], lw["a0"], lw["a2"], lw["g2"],
                lw["k_k"], lw["k_a"], lw["r_k"], lw["ln_w"], lw["ln_b"]] + cargs
        out_specs, out_shape = [out_spec, out_spec], [out_sds, out_sds]
    return pl.pallas_call(
        functools.partial(_rwkv_kernel, tb=tb, has_vres=has_vres),
        grid=(bsz, nb),
        in_specs=in_specs,
        out_specs=out_specs,
        out_shape=out_shape,
        scratch_shapes=scratch + chunk_scratch,
        compiler_params=_params("parallel", "arbitrary"),
        name="rwkv7",
    )(*args)


def _constants():
    lane_head = jnp.arange(GROUP_W) // HEAD_DIM
    hm = (lane_head[None, :] == jnp.arange(N_HEADS)[:, None]).astype(F32)
    bd = (lane_head[:, None] == lane_head[None, :])
    seg = bd.astype(BF16)
    pos = jnp.arange(GROUP_W) % RW_CHUNK
    hm4 = bd.astype(F32)
    sl = (bd & (pos[None, :] < pos[:, None])).astype(F32)
    il = (bd & (pos[None, :] <= pos[:, None])).astype(F32)
    eye = jnp.eye(GROUP_W, dtype=F32)
    t = jnp.arange(RW_CHUNK)
    ltri = (t[None, :] <= t[:, None]).astype(BF16)
    j = jnp.arange(SB_BLOCK)
    sb_cumt = -(j[None, :] > j[:, None]).astype(BF16)
    sb_neg1 = -jnp.ones((SUBLANES, SB_BLOCK), BF16)
    sb_hm4 = jnp.repeat(hm, SB_Q, axis=0).astype(BF16)
    return dict(hm=hm, bd=bd.astype(F32), seg=seg, hm4=hm4, sl=sl, il=il, eye=eye, ltri=ltri,
                sb_cumt=sb_cumt, sb_neg1=sb_neg1, sb_hm4=sb_hm4)


def kernel(x, norm_mix_pre, norm_mix_post, norm_ffn_pre, norm_ffn_post, w_in_first, w_in_rest, w_out,
           s5_a_re, s5_a_im, s5_log_dt, s5_b_re, s5_b_im, s5_c_re, s5_c_im, s5_d, s5_glu_w1, s5_glu_w2,
           rw_mu, rw_vres_mu, rw_w0, rw_w2, rw_a0, rw_a2, rw_g2, rw_v0, rw_v2,
           rw_k_k, rw_k_a, rw_r_k, rw_ln_w, rw_ln_b, w_up, w_down):
    bsz, seq, _ = x.shape
    consts = _constants()
    ret_tabs = _retention_tables(seq)
    x2 = x.reshape(bsz * seq, D_MODEL)
    row = lambda vec: vec.reshape(1, -1)
    zpad = lambda m, n: jnp.pad(m, ((0, n - m.shape[0]), (0, 0)))
    v_first = None
    for l in range(DEPTH):
        w_in = w_in_first if l == 0 else w_in_rest[l - 1]
        parts = [w_in[:, OFF_RW:N_IN0], w_in[:, OFF_RET:OFF_SB], w_in[:, :OFF_RET]]
        if l > 0:
            parts.append(jnp.pad(w_in[:, N_IN0:], ((0, 0), (0, GROUP_W - RWKV_LORA_V))))
        n32 = sum(p.shape[1] for p in parts)
        w_cat = jnp.concatenate(parts + [w_in[:, OFF_SB:OFF_RW]], axis=1).astype(BF16)
        p32, p16 = _in_proj(x2, row(norm_mix_pre[l]), w_cat, n32, 3 * GROUP_W)

        wb, wc, pw = _s5_tables(s5_a_re[l], s5_a_im[l], s5_log_dt[l], s5_b_re[l], s5_b_im[l],
                                s5_c_re[l], s5_c_im[l])
        out_s5 = _s5(p32, bsz, seq, wb, wc, pw, row(s5_d[l]),
                     s5_glu_w1[l].astype(BF16), s5_glu_w2[l].astype(BF16))
        out_ret = _retention(p32, bsz, seq, ret_tabs, consts)
        out_sb = _stick_breaking(p16, bsz, seq, consts)

        lw = dict(
            mu=row(rw_mu[l]), w0=row(rw_w0[l]), a0=row(rw_a0[l]),
            w2=zpad(rw_w2[l], 128).astype(BF16),
            a2=jnp.pad(rw_a2[l], ((RWKV_LORA_W, 0), (0, 0))).astype(BF16),
            g2=rw_g2[l].astype(BF16),
            k_k=row(rw_k_k[l]), k_a=row(rw_k_a[l]), r_k=row(rw_r_k[l]),
            ln_w=row(rw_ln_w[l]), ln_b=row(rw_ln_b[l]))
        if l == 0:
            out_rw, v_first = _rwkv(p32, None, bsz, seq, lw, consts, False)
        else:
            lw.update(vmu=row(jnp.pad(rw_vres_mu[l - 1], (0, GROUP_W - RWKV_LORA_V))),
                      v0=row(rw_v0[l - 1]), v2=zpad(rw_v2[l - 1], GROUP_W).astype(BF16))
            out_rw = _rwkv(p32, v_first, bsz, seq, lw, consts, True)

        x2 = _post((out_s5, out_ret, out_rw, out_sb), x2, w_out[l].astype(BF16), row(norm_mix_post[l]),
                   row(norm_ffn_pre[l]), w_up[l].astype(BF16), w_down[l].astype(BF16), row(norm_ffn_post[l]))
    return x2.reshape(bsz, seq, D_MODEL)
```

```python
import functools
import math

import jax
import jax.numpy as jnp
from jax import lax
from jax.experimental import pallas as pl
from jax.experimental.pallas import tpu as pltpu

F32 = jnp.float32
BF16 = jnp.bfloat16

D_MODEL = 1024
DEPTH = 4
GROUP_W = 256
HEAD_DIM = 64
N_HEADS = 4
S5_GROUP_CH = 16
S5_GROUPS = 16
S5_STATE = 64
S5_LANES = S5_GROUPS * S5_STATE
RET_CHUNK = 128
ROPE_BASE = 10000.0
RWKV_LORA_W = 64
RWKV_LORA_A = 64
RWKV_LORA_V = 32
RWKV_LORA_G = 128
RWKV_GN_EPS = 64e-5
RW_CHUNK = 64
SB_BLOCK = 128
SB_Q = 512
SB_DEAD = -110.0
SB_MASKED = -1e30
D_FF = 4 * D_MODEL
NORM_EPS = 1e-6

OFF_RET = GROUP_W
OFF_SB = OFF_RET + 4 * GROUP_W
OFF_RW = OFF_SB + 3 * GROUP_W
N_IN0 = OFF_RW + 4 * GROUP_W

P32_RET = 4
P32_S5 = 8
P32_VRES = 9

SUBLANES = 8
VMEM_LIMIT = 56 * 1024 * 1024

NN = (((1,), (0,)), ((), ()))
NT = (((1,), (1,)), ((), ()))
TN = (((0,), (0,)), ((), ()))


def _dot(a, b, dims=NN):
    return lax.dot_general(a, b, dims, preferred_element_type=F32)


def _split(x):
    hi = x.astype(BF16)
    lo = (x - hi.astype(F32)).astype(BF16)
    return hi, lo


def _mm2(a, b_exact, dims=NN):
    ah, al = _split(a)
    return _dot(ah, b_exact, dims) + _dot(al, b_exact, dims)


def _mm2l(a_exact, b, dims=NN):
    bh, bl = _split(b)
    return _dot(a_exact, bh, dims) + _dot(a_exact, bl, dims)


def _mm1(a, b, dims=NN):
    return _dot(a.astype(BF16), b.astype(BF16), dims)


def _softplus(x):
    return jnp.maximum(x, 0.0) + jnp.log(1.0 + jnp.exp(-jnp.abs(x)))


def _sigmoid(x):
    return 1.0 / (1.0 + jnp.exp(-x))


def _rms(x, gain):
    return x * lax.rsqrt(jnp.mean(x * x, axis=-1, keepdims=True) + NORM_EPS) * gain


def _params(*sem):
    return pltpu.CompilerParams(dimension_semantics=sem, vmem_limit_bytes=VMEM_LIMIT)


def _const_spec(shape):
    zeros = (0,) * len(shape)
    return pl.BlockSpec(shape, lambda *_: zeros)


def _in_proj_kernel(x_ref, g_ref, w_ref, o32_ref, o16_ref, *, n32, n16, tn):
    h = _rms(x_ref[...], g_ref[...]).astype(BF16)
    for c0 in range(0, n32, tn):
        o32_ref[:, c0:c0 + tn] = _dot(h, w_ref[:, c0:c0 + tn])
    for c0 in range(0, n16, tn):
        o16_ref[:, c0:c0 + tn] = _dot(h, w_ref[:, n32 + c0:n32 + c0 + tn]).astype(BF16)


def _in_proj(x2, gain, w, n32, n16, tm=512, tn=256):
    t = x2.shape[0]
    kern = functools.partial(_in_proj_kernel, n32=n32, n16=n16, tn=tn)
    return pl.pallas_call(
        kern,
        grid=(t // tm,),
        in_specs=[pl.BlockSpec((tm, D_MODEL), lambda i: (i, 0)),
                  _const_spec((1, D_MODEL)),
                  _const_spec((D_MODEL, n32 + n16))],
        out_specs=[pl.BlockSpec((tm, n32), lambda i: (i, 0)),
                   pl.BlockSpec((tm, n16), lambda i: (i, 0))],
        out_shape=[jax.ShapeDtypeStruct((t, n32), F32),
                   jax.ShapeDtypeStruct((t, n16), BF16)],
        compiler_params=_params("parallel"),
        name="in_proj",
    )(x2, gain, w)


def _post_kernel(m0_ref, m1_ref, m2_ref, m3_ref, x_ref, wo_ref, gmix_ref, gpre_ref,
                 wup_ref, wdn_ref, gpost_ref, o_ref, acc_ref, *, tf):
    mixed = (_mm1(m0_ref[...], wo_ref[0:GROUP_W, :])
             + _mm1(m1_ref[...], wo_ref[GROUP_W:2 * GROUP_W, :])
             + _mm1(m2_ref[...], wo_ref[2 * GROUP_W:3 * GROUP_W, :])
             + _mm1(m3_ref[...], wo_ref[3 * GROUP_W:4 * GROUP_W, :]))
    x1 = x_ref[...] + _rms(mixed, gmix_ref[...])
    h = _rms(x1, gpre_ref[...]).astype(BF16)
    for c0 in range(0, D_FF, tf):
        a = jnp.maximum(_dot(h, wup_ref[:, c0:c0 + tf]), 0.0)
        part = _dot((a * a).astype(BF16), wdn_ref[c0:c0 + tf, :])
        if c0 == 0:
            acc_ref[...] = part
        else:
            acc_ref[...] += part
    o_ref[...] = x1 + _rms(acc_ref[...], gpost_ref[...])


def _post(mix_outs, x2, wo, gmix, gpre, wup, wdn, gpost, tm=512, tf=512):
    t = x2.shape[0]
    row = lambda w: pl.BlockSpec((tm, w), lambda i: (i, 0))
    single = lambda shape: pl.BlockSpec(shape, lambda i: (0, 0), pipeline_mode=pl.Buffered(1))
    return pl.pallas_call(
        functools.partial(_post_kernel, tf=tf),
        grid=(t // tm,),
        in_specs=[row(GROUP_W)] * 4 + [row(D_MODEL), single((D_MODEL, D_MODEL)),
                  _const_spec((1, D_MODEL)), _const_spec((1, D_MODEL)),
                  single((D_MODEL, D_FF)), single((D_FF, D_MODEL)), _const_spec((1, D_MODEL))],
        out_specs=row(D_MODEL),
        out_shape=jax.ShapeDtypeStruct((t, D_MODEL), F32),
        scratch_shapes=[pltpu.VMEM((tm, D_MODEL), F32)],
        compiler_params=_params("parallel"),
        name="out_proj_ffn",
    )(*mix_outs, x2, wo, gmix, gpre, wup, wdn, gpost)


def _s5_kernel(u_ref, wb_ref, wc_ref, pw_ref, d_ref, w1_ref, w2_ref, o_ref,
               st_ref, xr_ref, xi_ref, *, tb):
    @pl.when(pl.program_id(1) == 0)
    def _():
        st_ref[...] = jnp.zeros_like(st_ref)

    u = u_ref[...]
    bu = _mm1(u, wb_ref[...])
    xr_ref[...] = bu[:, :S5_LANES]
    xi_ref[...] = bu[:, S5_LANES:]

    def tile(j, carry):
        cr, ci = carry
        r0 = pl.multiple_of(j * SUBLANES, SUBLANES)
        xr = xr_ref[pl.ds(r0, SUBLANES), :]
        xi = xi_ref[pl.ds(r0, SUBLANES), :]
        for n, shift in enumerate((1, 2, 4)):
            pr = pw_ref[2 * n]
            pi = pw_ref[2 * n + 1]
            sr = pltpu.roll(xr, shift, axis=0)
            si = pltpu.roll(xi, shift, axis=0)
            xr, xi = xr + pr * sr - pi * si, xi + pr * si + pi * sr
        pr = pw_ref[6]
        pi = pw_ref[7]
        xr, xi = xr + pr * cr - pi * ci, xi + pr * ci + pi * cr
        xr_ref[pl.ds(r0, SUBLANES), :] = xr
        xi_ref[pl.ds(r0, SUBLANES), :] = xi
        return xr[SUBLANES - 1:SUBLANES, :], xi[SUBLANES - 1:SUBLANES, :]

    cr, ci = lax.fori_loop(0, tb // SUBLANES, tile, (st_ref[0:1, :], st_ref[1:2, :]))
    st_ref[0:1, :] = cr
    st_ref[1:2, :] = ci

    y = (_mm1(xr_ref[...], wc_ref[0:S5_LANES, :]) + _mm1(xi_ref[...], wc_ref[S5_LANES:, :])
         + d_ref[...] * u)
    y = jax.nn.gelu(y).astype(BF16)
    o_ref[...] = _dot(y, w1_ref[...]) * _sigmoid(_dot(y, w2_ref[...]))


def _s5(p32, bsz, seq, wb, wc, pw, d, w1, w2, tb=512):
    nb = seq // tb
    return pl.pallas_call(
        functools.partial(_s5_kernel, tb=tb),
        grid=(bsz, nb),
        in_specs=[pl.BlockSpec((tb, GROUP_W), lambda b, i: (b * nb + i, P32_S5)),
                  _const_spec((GROUP_W, 2 * S5_LANES)),
                  _const_spec((2 * S5_LANES, GROUP_W)),
                  _const_spec((8, SUBLANES, S5_LANES)),
                  _const_spec((1, GROUP_W)),
                  _const_spec((GROUP_W, GROUP_W)),
                  _const_spec((GROUP_W, GROUP_W))],
        out_specs=pl.BlockSpec((tb, GROUP_W), lambda b, i: (b * nb + i, 0)),
        out_shape=jax.ShapeDtypeStruct((bsz * seq, GROUP_W), F32),
        scratch_shapes=[pltpu.VMEM((SUBLANES, S5_LANES), F32),
                        pltpu.VMEM((tb, S5_LANES), F32),
                        pltpu.VMEM((tb, S5_LANES), F32)],
        compiler_params=_params("parallel", "arbitrary"),
        name="s5",
    )(p32, wb, wc, pw, d, w1, w2)


def _s5_tables(a_re, a_im, log_dt, b_re, b_im, c_re, c_im):
    dt = jnp.exp(log_dt)[:, None]
    mag = jnp.exp(a_re * dt)
    ab_re = mag * jnp.cos(a_im * dt)
    ab_im = mag * jnp.sin(a_im * dt)
    den = a_re * a_re + a_im * a_im
    num_re = ab_re - 1.0
    zoh_re = (num_re * a_re + ab_im * a_im) / den
    zoh_im = (ab_im * a_re - num_re * a_im) / den
    bb_re = zoh_re[..., None] * b_re - zoh_im[..., None] * b_im
    bb_im = zoh_re[..., None] * b_im + zoh_im[..., None] * b_re
    eye = jnp.eye(S5_GROUPS, dtype=F32)
    blk_in = lambda m: jnp.einsum("gpc,gh->gchp", m, eye).reshape(GROUP_W, S5_LANES)
    blk_out = lambda m: jnp.einsum("gcp,gh->gphc", m, eye).reshape(S5_LANES, GROUP_W)
    wb = jnp.concatenate([blk_in(bb_re), blk_in(bb_im)], axis=1).astype(BF16)
    wc = jnp.concatenate([blk_out(c_re), -blk_out(c_im)], axis=0).astype(BF16)

    def power(n):
        m = jnp.exp(n * (a_re * dt)[None])
        ang = n * (a_im * dt)[None]
        return ((m * jnp.cos(ang)).reshape(-1, S5_LANES), (m * jnp.sin(ang)).reshape(-1, S5_LANES))

    rows = jnp.arange(SUBLANES, dtype=F32)[:, None, None]
    tabs = []
    for shift in (1, 2, 4):
        pr, pi = power(jnp.full_like(rows, float(shift)))
        keep = (jnp.arange(SUBLANES) >= shift)[:, None]
        tabs += [jnp.where(keep, pr, 0.0), jnp.where(keep, pi, 0.0)]
    pr, pi = power(rows + 1.0)
    tabs += [pr, pi]
    return wb, wc, jnp.stack(tabs)


def _rope(t, c, s_up, s_dn):
    return (t * c + pltpu.roll(t, HEAD_DIM // 2, axis=1) * s_up
            + pltpu.roll(t, GROUP_W - HEAD_DIM // 2, axis=1) * s_dn)


def _ret_kernel(q_ref, k_ref, v_ref, g_ref, cos_ref, sup_ref, sdn_ref, intra_ref, qd_ref, kd_ref,
                cd_ref, hm_ref, bd_ref, seg_ref, o_ref, st_ref, *, tb):
    @pl.when(pl.program_id(1) == 0)
    def _():
        st_ref[...] = jnp.zeros_like(st_ref)

    for c in range(tb // RET_CHUNK):
        rows = slice(c * RET_CHUNK, (c + 1) * RET_CHUNK)
        cos = cos_ref[rows, :]
        sup = sup_ref[rows, :]
        sdn = sdn_ref[rows, :]
        q = _rope(q_ref[rows, :], cos, sup, sdn)
        k = _rope(k_ref[rows, :], cos, sup, sdn) * HEAD_DIM ** -0.5
        v = v_ref[rows, :]
        kb = k.astype(BF16)
        vb = v.astype(BF16)
        state = st_ref[...]
        o = _mm1(q * qd_ref[...], state)
        for h in range(N_HEADS):
            hm = hm_ref[h:h + 1, :]
            scores = _dot((q * hm).astype(BF16), kb, NT) * intra_ref[h]
            o = o + hm * _dot(scores.astype(BF16), vb)
        st_ref[...] = state * cd_ref[...] + bd_ref[...] * _dot((k * kd_ref[...]).astype(BF16), vb, TN)
        ms = _mm2(o * o, seg_ref[...]) * (1.0 / HEAD_DIM)
        o = o * lax.rsqrt(ms + NORM_EPS)
        g = g_ref[rows, :]
        o_ref[rows, :] = g * _sigmoid(g) * o


def _retention(p32, bsz, seq, tabs, consts, tb=512):
    nb = seq // tb
    col = lambda j: pl.BlockSpec((tb, GROUP_W), lambda b, i: (b * nb + i, j))
    pos = pl.BlockSpec((tb, GROUP_W), lambda b, i: (i, 0))
    cos, sup, sdn, intra, qd, kd, cd = tabs
    return pl.pallas_call(
        functools.partial(_ret_kernel, tb=tb),
        grid=(bsz, nb),
        in_specs=[col(P32_RET), col(P32_RET + 1), col(P32_RET + 2), col(P32_RET + 3), pos, pos, pos,
                  _const_spec((N_HEADS, RET_CHUNK, RET_CHUNK)),
                  _const_spec((RET_CHUNK, GROUP_W)), _const_spec((RET_CHUNK, GROUP_W)),
                  _const_spec((GROUP_W, GROUP_W)), _const_spec((N_HEADS, GROUP_W)),
                  _const_spec((GROUP_W, GROUP_W)), _const_spec((GROUP_W, GROUP_W))],
        out_specs=pl.BlockSpec((tb, GROUP_W), lambda b, i: (b * nb + i, 0)),
        out_shape=jax.ShapeDtypeStruct((bsz * seq, GROUP_W), F32),
        scratch_shapes=[pltpu.VMEM((GROUP_W, GROUP_W), F32)],
        compiler_params=_params("parallel", "arbitrary"),
        name="retention",
    )(p32, p32, p32, p32, cos, sup, sdn, intra, qd, kd, cd, consts["hm"], consts["bd"], consts["seg"])


def _retention_tables(seq):
    inv_freq = ROPE_BASE ** (-jnp.arange(0, HEAD_DIM, 2, dtype=F32) / HEAD_DIM)
    ang = jnp.arange(seq, dtype=F32)[:, None] * inv_freq[None, :]
    cos, sin = jnp.cos(ang), jnp.sin(ang)
    zero = jnp.zeros_like(sin)
    cos_t = jnp.tile(jnp.concatenate([cos, cos], axis=1), (1, N_HEADS))
    sup_t = jnp.tile(jnp.concatenate([zero, sin], axis=1), (1, N_HEADS))
    sdn_t = jnp.tile(jnp.concatenate([-sin, zero], axis=1), (1, N_HEADS))
    log_gamma = jnp.log1p(-jnp.exp2(-5.0 - jnp.arange(N_HEADS, dtype=F32)))
    idx = jnp.arange(RET_CHUNK, dtype=F32)
    rel = idx[:, None] - idx[None, :]
    intra = jnp.where(rel >= 0, jnp.exp(log_gamma[:, None, None] * jnp.maximum(rel, 0.0)), 0.0)
    lanes = jnp.repeat(log_gamma, HEAD_DIM)[None, :]
    qd = jnp.exp(lanes * (idx + 1.0)[:, None])
    kd = jnp.exp(lanes * (RET_CHUNK - 1.0 - idx)[:, None])
    cd = jnp.broadcast_to(jnp.exp(lanes * RET_CHUNK), (GROUP_W, GROUP_W))
    return cos_t, sup_t, sdn_t, intra, qd, kd, cd


def _sb_kernel(q_ref, k_ref, vt_ref, hm4_ref, cumt_ref, neg1_ref, trib_ref, trin_ref, o_ref, acc_ref):
    qb = pl.program_id(1)
    nsub = SB_Q // SB_BLOCK
    q = q_ref[...] * HEAD_DIM ** -0.5
    q4 = _tile4(q) * hm4_ref[...]
    cumt = cumt_ref[...]
    neg1 = neg1_ref[...]
    acc_ref[...] = jnp.zeros_like(acc_ref)
    heads = range(N_HEADS)
    dims = lambda h: slice(h * HEAD_DIM, (h + 1) * HEAD_DIM)

    def first_cols(x, f):
        head = f(x[:, :SB_BLOCK])
        return head if x.shape[1] == SB_BLOCK else jnp.concatenate([head, x[:, SB_BLOCK:]], axis=1)

    def add_blocks(tiles, runs):
        runs = list(runs)
        logits = []
        for blk, lo, _ in tiles:
            s0 = pl.multiple_of(blk * SB_BLOCK, SB_BLOCK)
            logits += [_dot(k_ref[pl.ds(s0, SB_BLOCK), :], q4[h * SB_Q + lo:(h + 1) * SB_Q, :], NT) for h in heads]
        pending = []
        for t, (blk, lo, diagonal) in enumerate(tiles):
            for h in heads:
                zt = logits[t * N_HEADS + h]
                sp = jnp.maximum(zt, 0.0) + jnp.log(1.0 + jnp.exp(-jnp.abs(zt)))
                log_w = zt - sp
                spb = sp.astype(BF16)
                if diagonal:
                    spb = first_cols(spb, lambda x: x * trib_ref[...])
                    log_w = first_cols(log_w, lambda x: x + trin_ref[...])
                run = runs[h][:, lo:]
                log_w = log_w + _dot(cumt, spb) + run
                run = run + _dot(neg1, spb)[0:1, :]
                runs[h] = run if lo == 0 else jnp.concatenate([runs[h][:, :lo], run], axis=1)
                pending.append((blk, h, lo, log_w))
        for blk, h, lo, log_w in pending:
            acc_ref[dims(h), lo:] += _dot(vt_ref[blk, dims(h), :], jnp.exp(log_w).astype(BF16))
        return tuple(runs)

    runs = tuple(jnp.zeros((1, SB_Q), F32) for _ in heads)
    top = qb * nsub
    for m in range(nsub - 1, 0, -2):
        runs = add_blocks([(top + m, m * SB_BLOCK, True), (top + m - 1, (m - 1) * SB_BLOCK, True)], runs)

    def live(state):
        return jnp.logical_and(state[0] < qb * (nsub // 2), state[1] > SB_DEAD)

    def pair(state):
        jj, runs = state[0], state[2:]
        blk = top - 1 - 2 * jj
        runs = add_blocks([(blk, 0, False), (blk - 1, 0, False)], runs)
        slowest = jnp.max(jnp.maximum(jnp.maximum(runs[0], runs[1]), jnp.maximum(runs[2], runs[3])))
        return (jj + 1, slowest) + runs

    lax.while_loop(live, pair, (jnp.int32(0), jnp.float32(0.0)) + runs)
    o_ref[...] = acc_ref[...].T


def _stick_breaking(p16, bsz, seq, consts):
    nq = seq // SB_Q
    nk = seq // SB_BLOCK
    vt = p16[:, 2 * GROUP_W:].reshape(bsz * nk, SB_BLOCK, GROUP_W).swapaxes(1, 2)
    return pl.pallas_call(
        _sb_kernel,
        grid=(bsz, nq),
        in_specs=[pl.BlockSpec((SB_Q, GROUP_W), lambda b, i: (b * nq + i, 0)),
                  pl.BlockSpec((seq, GROUP_W), lambda b, i: (b, 1)),
                  pl.BlockSpec((nk, GROUP_W, SB_BLOCK), lambda b, i: (b, 0, 0)),
                  _const_spec((N_HEADS * SB_Q, GROUP_W)),
                  _const_spec((SB_BLOCK, SB_BLOCK)),
                  _const_spec((SUBLANES, SB_BLOCK)),
                  _const_spec((SB_BLOCK, SB_BLOCK)),
                  _const_spec((SB_BLOCK, SB_BLOCK))],
        out_specs=pl.BlockSpec((SB_Q, GROUP_W), lambda b, i: (b * nq + i, 0)),
        out_shape=jax.ShapeDtypeStruct((bsz * seq, GROUP_W), F32),
        scratch_shapes=[pltpu.VMEM((GROUP_W, SB_Q), F32)],
        compiler_params=_params("parallel", "arbitrary"),
        name="stick_breaking",
    )(p16, p16, vt, consts["sb_hm4"], consts["sb_cumt"], consts["sb_neg1"], consts["sb_trib"], consts["sb_trin"])


def _tile4(x):
    return jnp.concatenate([x, x, x, x], axis=0)


def _fold4(x):
    n = x.shape[0] // N_HEADS
    return x[0:n] + x[n:2 * n] + x[2 * n:3 * n] + x[3 * n:4 * n]


def _shifted(cur, prev_ref, first_row):
    sh = jnp.where(first_row, prev_ref[0:1, :], pltpu.roll(cur, 1, axis=0))
    prev_ref[0:1, :] = cur[cur.shape[0] - 1:, :]
    return sh


def _rwkv_kernel(*refs, tb, has_vres):
    if has_vres:
        (cols_ref, vres_ref, vfirst_ref, mu_ref, vmu_ref, w0_ref, w2_ref, a0_ref, a2_ref, g2_ref,
         v0_ref, v2_ref, kk_ref, ka_ref, rk_ref, lnw_ref, lnb_ref,
         hm4_ref, sl_ref, il_ref, eye_ref, ltri_ref, seg_ref,
         o_ref, z_ref, prev_ref, prevv_ref, r_s, k_s, v_s, a_s, b_s, lw_s, y_s) = refs
    else:
        (cols_ref, mu_ref, w0_ref, w2_ref, a0_ref, a2_ref, g2_ref,
         kk_ref, ka_ref, rk_ref, lnw_ref, lnb_ref,
         hm4_ref, sl_ref, il_ref, eye_ref, ltri_ref, seg_ref,
         o_ref, vout_ref, z_ref, prev_ref, r_s, k_s, v_s, a_s, b_s, lw_s, y_s) = refs

    @pl.when(pl.program_id(1) == 0)
    def _():
        z_ref[...] = jnp.zeros_like(z_ref)
        prev_ref[...] = jnp.zeros_like(prev_ref)
        if has_vres:
            prevv_ref[...] = jnp.zeros_like(prevv_ref)

    first_row = lax.broadcasted_iota(jnp.int32, (tb, 1), 0) == 0
    cols = cols_ref[...]
    xs = cols + (_shifted(cols, prev_ref, first_row) - cols) * mu_ref[...]
    r = xs[:, 0:GROUP_W]
    k = xs[:, GROUP_W:2 * GROUP_W]
    v = xs[:, 2 * GROUP_W:3 * GROUP_W]
    wa = xs[:, 3 * GROUP_W:3 * GROUP_W + 128]
    gd = xs[:, 3 * GROUP_W + 128:4 * GROUP_W]
    w_log = -_softplus(-(w0_ref[...] + _mm1(jnp.tanh(wa), w2_ref[...]))) - 0.5
    lw = -jnp.exp(w_log)
    a = _sigmoid(a0_ref[...] + _mm1(wa, a2_ref[...]))
    g = _mm1(_sigmoid(gd), g2_ref[...])
    if has_vres:
        vr = vres_ref[...]
        vx = vr + (_shifted(vr, prevv_ref, first_row) - vr) * vmu_ref[...]
        v = v + (vfirst_ref[...] - v) * _sigmoid(v0_ref[...] + _mm1(vx, v2_ref[...]))
    else:
        vout_ref[...] = v
    seg = seg_ref[...]
    kk = k * kk_ref[...]
    kk = kk * lax.rsqrt(jnp.maximum(_mm2(kk * kk, seg), 1e-12))
    k = k * (1.0 + (a - 1.0) * ka_ref[...])
    r_s[...] = r
    k_s[...] = k
    v_s[...] = v
    a_s[...] = -kk
    b_s[...] = kk * a
    lw_s[...] = lw

    hm4 = hm4_ref[...]
    sl = sl_ref[...]
    il = il_ref[...]
    eye = eye_ref[...]

    w256 = GROUP_W
    chunks = range(tb // RW_CHUNK)
    each = lambda f, *lists: [f(*xs) for xs in zip(*lists)]
    take = lambda ref: [ref[ci * RW_CHUNK:(ci + 1) * RW_CHUNK, :] for ci in chunks]
    rc, kc, vc, ac, bc, lwc = take(r_s), take(k_s), take(v_s), take(a_s), take(b_s), take(lw_s)
    ltri = ltri_ref[...]
    c = each(lambda x: _mm2l(ltri, x), lwc)
    c_end = each(lambda x: x[RW_CHUNK - 1:, :], c)
    at4 = each(lambda a_, c_, l_: _tile4(a_ * jnp.exp(c_ - l_)) * hm4, ac, c, lwc)
    rt4 = each(lambda r_, c_: _tile4(r_ * jnp.exp(c_)) * hm4, rc, c)
    at4b = each(lambda x: x.astype(BF16), at4)
    be4b = each(lambda b_, c_, e_: (_tile4(b_ * jnp.exp(e_ - c_)) * hm4).astype(BF16), bc, c, c_end)
    ke4b = each(lambda k_, c_, e_: (_tile4(k_ * jnp.exp(e_ - c_)) * hm4).astype(BF16), kc, c, c_end)
    v4b = each(lambda v_: (_tile4(v_) * hm4).astype(BF16), vc)
    lhs = each(lambda a_, r_: jnp.concatenate([a_, r_.astype(BF16)], axis=0), at4b, rt4)
    rhs = each(lambda b_, k_, c_: jnp.concatenate([_tile4((b_ * jnp.exp(-c_)).astype(BF16)),
                                                    _tile4((k_ * jnp.exp(-c_)).astype(BF16))], axis=0), bc, kc, c)
    nn = each(lambda l_, r_: _dot(l_, r_, NT), lhs, rhs)
    n_ab = each(lambda x: x[:w256, :w256] * sl, nn)
    n_ak = each(lambda x: (x[:w256, w256:] * sl).astype(BF16), nn)
    n_rb = each(lambda x: (x[w256:, :w256] * il).astype(BF16), nn)
    n_rk = each(lambda x: (x[w256:, w256:] * il).astype(BF16), nn)
    inv = each(lambda x: eye + x, n_ab)
    pw = each(lambda x: x.astype(BF16), n_ab)
    for _ in range(int(math.log2(RW_CHUNK)) - 1):
        pw = each(lambda x: _dot(x, x).astype(BF16), pw)
        inv = each(lambda i_, p_: i_ + _dot(p_, i_.astype(BF16)), inv, pw)
    akv = each(lambda n_, v_: _dot(n_, v_).astype(BF16), n_ak, v4b)
    ugb = each(lambda i_, x_, a_: _dot(i_.astype(BF16), jnp.concatenate([x_, a_], axis=1)).astype(BF16),
               inv, akv, at4b)
    yh = each(_dot, n_rb, ugb)
    y0 = each(lambda y_, n_, v_: y_[:, :w256] + _dot(n_, v_), yh, n_rk, v4b)
    hmat = each(lambda r_, y_: (r_ + y_[:, w256:]).astype(BF16), rt4, yh)
    qp = each(lambda b_, u_: _dot(b_, u_, TN), be4b, ugb)
    qm = each(lambda q_, k_, v_: q_[:, :w256] + _dot(k_, v_, TN), qp, ke4b, v4b)
    p = each(lambda e_, q_: (eye * jnp.exp(e_) + q_[:, w256:]).astype(BF16), c_end, qp)
    z = z_ref[...]
    for ci in chunks:
        zb = z.astype(BF16)
        y_s[ci * RW_CHUNK:(ci + 1) * RW_CHUNK, :] = _fold4(y0[ci] + _dot(hmat[ci], zb))
        z = _dot(p[ci], zb) + qm[ci]
    z_ref[...] = z

    y = y_s[...]
    inv_n = 1.0 / HEAD_DIM
    mean = _mm2(y, seg) * inv_n
    yc = y - mean
    var = _mm2(yc * yc, seg) * inv_n
    y = yc * lax.rsqrt(var + RWKV_GN_EPS) * lnw_ref[...] + lnb_ref[...]
    r = r_s[...]
    k = k_s[...]
    bonus = _mm2(r * k * rk_ref[...], seg) * v_s[...]
    o_ref[...] = (y + bonus) * g


def _rwkv(p32, vfirst, bsz, seq, lw, consts, has_vres, tb=256):
    nb = seq // tb
    rows = lambda w, j: pl.BlockSpec((tb, w), lambda b, i: (b * nb + i, j))
    vec = _const_spec((1, GROUP_W))
    mat = lambda n: _const_spec((n, GROUP_W))
    big = _const_spec((GROUP_W, GROUP_W))
    cnames = ("hm4", "sl", "il", "eye", "ltri", "seg")
    cspecs = [big, big, big, big, _const_spec((RW_CHUNK, RW_CHUNK)), big]
    cargs = [consts[n] for n in cnames]
    cols_spec = pl.BlockSpec((tb, 4 * GROUP_W), lambda b, i: (b * nb + i, 0))
    out_spec = pl.BlockSpec((tb, GROUP_W), lambda b, i: (b * nb + i, 0))
    out_sds = jax.ShapeDtypeStruct((bsz * seq, GROUP_W), F32)
    scratch = [pltpu.VMEM((GROUP_W, GROUP_W), F32), pltpu.VMEM((SUBLANES, 4 * GROUP_W), F32)]
    chunk_scratch = [pltpu.VMEM((tb, GROUP_W), F32)] * 7
    if has_vres:
        in_specs = ([cols_spec, rows(GROUP_W, P32_VRES), rows(GROUP_W, 0), _const_spec((1, 4 * GROUP_W)), vec,
                     vec, mat(128), vec, mat(128), mat(RWKV_LORA_G), vec, mat(GROUP_W), vec, vec, vec, vec, vec]
                    + cspecs)
        args = [p32, p32, vfirst, lw["mu"], lw["vmu"], lw["w0"], lw["w2"], lw["a0"], lw["a2"],
                lw["g2"], lw["v0"], lw["v2"], lw["k_k"], lw["k_a"], lw["r_k"], lw["ln_w"], lw["ln_b"]] + cargs
        out_specs, out_shape = out_spec, out_sds
        scratch = scratch + [pltpu.VMEM((SUBLANES, GROUP_W), F32)]
    else:
        in_specs = ([cols_spec, _const_spec((1, 4 * GROUP_W)), vec, mat(128), vec, mat(128), mat(RWKV_LORA_G),
                     vec, vec, vec, vec, vec] + cspecs)
        args = [p32, lw["mu"], lw["w0"], lw["w2"], lw["a0"], lw["a2"], lw["g2"],
                lw["k_k"], lw["k_a"], lw["r_k"], lw["ln_w"], lw["ln_b"]] + cargs
        out_specs, out_shape = [out_spec, out_spec], [out_sds, out_sds]
    return pl.pallas_call(
        functools.partial(_rwkv_kernel, tb=tb, has_vres=has_vres),
        grid=(bsz, nb),
        in_specs=in_specs,
        out_specs=out_specs,
        out_shape=out_shape,
        scratch_shapes=scratch + chunk_scratch,
        compiler_params=_params("parallel", "arbitrary"),
        name="rwkv7",
    )(*args)


def _constants():
    lane_head = jnp.arange(GROUP_W) // HEAD_DIM
    hm = (lane_head[None, :] == jnp.arange(N_HEADS)[:, None]).astype(F32)
    bd = (lane_head[:, None] == lane_head[None, :])
    seg = bd.astype(BF16)
    pos = jnp.arange(GROUP_W) % RW_CHUNK
    hm4 = bd.astype(F32)
    sl = (bd & (pos[None, :] < pos[:, None])).astype(F32)
    il = (bd & (pos[None, :] <= pos[:, None])).astype(F32)
    eye = jnp.eye(GROUP_W, dtype=F32)
    t = jnp.arange(RW_CHUNK)
    ltri = (t[None, :] <= t[:, None]).astype(BF16)
    j = jnp.arange(SB_BLOCK)
    sb_cumt = -(j[None, :] > j[:, None]).astype(BF16)
    sb_neg1 = -jnp.ones((SUBLANES, SB_BLOCK), BF16)
    sb_hm4 = jnp.repeat(hm, SB_Q, axis=0).astype(BF16)
    visible = j[:, None] < j[None, :]
    sb_trib = visible.astype(BF16)
    sb_trin = jnp.where(visible, 0.0, SB_MASKED).astype(F32)
    return dict(hm=hm, bd=bd.astype(F32), seg=seg, hm4=hm4, sl=sl, il=il, eye=eye, ltri=ltri,
                sb_cumt=sb_cumt, sb_neg1=sb_neg1, sb_hm4=sb_hm4, sb_trib=sb_trib, sb_trin=sb_trin)


def kernel(x, norm_mix_pre, norm_mix_post, norm_ffn_pre, norm_ffn_post, w_in_first, w_in_rest, w_out,
           s5_a_re, s5_a_im, s5_log_dt, s5_b_re, s5_b_im, s5_c_re, s5_c_im, s5_d, s5_glu_w1, s5_glu_w2,
           rw_mu, rw_vres_mu, rw_w0, rw_w2, rw_a0, rw_a2, rw_g2, rw_v0, rw_v2,
           rw_k_k, rw_k_a, rw_r_k, rw_ln_w, rw_ln_b, w_up, w_down):
    bsz, seq, _ = x.shape
    consts = _constants()
    ret_tabs = _retention_tables(seq)
    x2 = x.reshape(bsz * seq, D_MODEL)
    row = lambda vec: vec.reshape(1, -1)
    zpad = lambda m, n: jnp.pad(m, ((0, n - m.shape[0]), (0, 0)))
    v_first = None
    for l in range(DEPTH):
        w_in = w_in_first if l == 0 else w_in_rest[l - 1]
        parts = [w_in[:, OFF_RW:N_IN0], w_in[:, OFF_RET:OFF_SB], w_in[:, :OFF_RET]]
        if l > 0:
            parts.append(jnp.pad(w_in[:, N_IN0:], ((0, 0), (0, GROUP_W - RWKV_LORA_V))))
        n32 = sum(p.shape[1] for p in parts)
        w_cat = jnp.concatenate(parts + [w_in[:, OFF_SB:OFF_RW]], axis=1).astype(BF16)
        p32, p16 = _in_proj(x2, row(norm_mix_pre[l]), w_cat, n32, 3 * GROUP_W)

        wb, wc, pw = _s5_tables(s5_a_re[l], s5_a_im[l], s5_log_dt[l], s5_b_re[l], s5_b_im[l],
                                s5_c_re[l], s5_c_im[l])
        out_s5 = _s5(p32, bsz, seq, wb, wc, pw, row(s5_d[l]),
                     s5_glu_w1[l].astype(BF16), s5_glu_w2[l].astype(BF16))
        out_ret = _retention(p32, bsz, seq, ret_tabs, consts)
        out_sb = _stick_breaking(p16, bsz, seq, consts)

        lw = dict(
            mu=row(rw_mu[l]), w0=row(rw_w0[l]), a0=row(rw_a0[l]),
            w2=zpad(rw_w2[l], 128).astype(BF16),
            a2=jnp.pad(rw_a2[l], ((RWKV_LORA_W, 0), (0, 0))).astype(BF16),
            g2=rw_g2[l].astype(BF16),
            k_k=row(rw_k_k[l]), k_a=row(rw_k_a[l]), r_k=row(rw_r_k[l]),
            ln_w=row(rw_ln_w[l]), ln_b=row(rw_ln_b[l]))
        if l == 0:
            out_rw, v_first = _rwkv(p32, None, bsz, seq, lw, consts, False)
        else:
            lw.update(vmu=row(jnp.pad(rw_vres_mu[l - 1], (0, GROUP_W - RWKV_LORA_V))),
                      v0=row(rw_v0[l - 1]), v2=zpad(rw_v2[l - 1], GROUP_W).astype(BF16))
            out_rw = _rwkv(p32, v_first, bsz, seq, lw, consts, True)

        x2 = _post((out_s5, out_ret, out_rw, out_sb), x2, w_out[l].astype(BF16), row(norm_mix_post[l]),
                   row(norm_ffn_pre[l]), w_up[l].astype(BF16), w_down[l].astype(BF16), row(norm_ffn_post[l]))
    return x2.reshape(bsz, seq, D_MODEL)
```

```python
import functools
import math

import jax
import jax.numpy as jnp
from jax import lax
from jax.experimental import pallas as pl
from jax.experimental.pallas import tpu as pltpu

F32 = jnp.float32
BF16 = jnp.bfloat16

D_MODEL = 1024
DEPTH = 4
GROUP_W = 256
HEAD_DIM = 64
N_HEADS = 4
S5_GROUP_CH = 16
S5_GROUPS = 16
S5_STATE = 64
S5_LANES = S5_GROUPS * S5_STATE
S5_SUB = 128
RET_CHUNK = 128
ROPE_BASE = 10000.0
RWKV_LORA_W = 64
RWKV_LORA_A = 64
RWKV_LORA_V = 32
RWKV_LORA_G = 128
RWKV_GN_EPS = 64e-5
RW_CHUNK = 64
RW_GROUP = 4
SB_BLOCK = 128
SB_Q = 512
SB_DEAD = -110.0
SB_MASKED = -1e30
D_FF = 4 * D_MODEL
NORM_EPS = 1e-6

OFF_RET = GROUP_W
OFF_SB = OFF_RET + 4 * GROUP_W
OFF_RW = OFF_SB + 3 * GROUP_W
N_IN0 = OFF_RW + 4 * GROUP_W

P32_RET = 4
P32_S5 = 8
P32_VRES = 9

SUBLANES = 8
VMEM_LIMIT = 56 * 1024 * 1024

NN = (((1,), (0,)), ((), ()))
NT = (((1,), (1,)), ((), ()))
TN = (((0,), (0,)), ((), ()))


def _dot(a, b, dims=NN):
    return lax.dot_general(a, b, dims, preferred_element_type=F32)


def _split(x):
    hi = x.astype(BF16)
    lo = (x - hi.astype(F32)).astype(BF16)
    return hi, lo


def _mm2(a, b_exact, dims=NN):
    ah, al = _split(a)
    return _dot(ah, b_exact, dims) + _dot(al, b_exact, dims)


def _mm2l(a_exact, b, dims=NN):
    bh, bl = _split(b)
    return _dot(a_exact, bh, dims) + _dot(a_exact, bl, dims)


def _mm1(a, b, dims=NN):
    return _dot(a.astype(BF16), b.astype(BF16), dims)


def _softplus(x):
    return jnp.maximum(x, 0.0) + jnp.log(1.0 + jnp.exp(-jnp.abs(x)))


def _sigmoid(x):
    return 1.0 / (1.0 + jnp.exp(-x))


def _rms(x, gain):
    return x * lax.rsqrt(jnp.mean(x * x, axis=-1, keepdims=True) + NORM_EPS) * gain


def _params(*sem):
    return pltpu.CompilerParams(dimension_semantics=sem, vmem_limit_bytes=VMEM_LIMIT)


def _const_spec(shape):
    zeros = (0,) * len(shape)
    return pl.BlockSpec(shape, lambda *_: zeros)


def _in_proj_kernel(x_ref, g_ref, w_ref, o32_ref, o16_ref, *, n32, n16, tn):
    h = _rms(x_ref[...], g_ref[...]).astype(BF16)
    for c0 in range(0, n32, tn):
        o32_ref[:, c0:c0 + tn] = _dot(h, w_ref[:, c0:c0 + tn])
    for c0 in range(0, n16, tn):
        o16_ref[:, c0:c0 + tn] = _dot(h, w_ref[:, n32 + c0:n32 + c0 + tn]).astype(BF16)


def _in_proj(x2, gain, w, n32, n16, tm=512, tn=256):
    t = x2.shape[0]
    kern = functools.partial(_in_proj_kernel, n32=n32, n16=n16, tn=tn)
    return pl.pallas_call(
        kern,
        grid=(t // tm,),
        in_specs=[pl.BlockSpec((tm, D_MODEL), lambda i: (i, 0)),
                  _const_spec((1, D_MODEL)),
                  _const_spec((D_MODEL, n32 + n16))],
        out_specs=[pl.BlockSpec((tm, n32), lambda i: (i, 0)),
                   pl.BlockSpec((tm, n16), lambda i: (i, 0))],
        out_shape=[jax.ShapeDtypeStruct((t, n32), F32),
                   jax.ShapeDtypeStruct((t, n16), BF16)],
        compiler_params=_params("parallel"),
        name="in_proj",
    )(x2, gain, w)


def _post_kernel(m0_ref, m1_ref, m2_ref, m3_ref, x_ref, wo_ref, gmix_ref, gpre_ref,
                 wup_ref, wdn_ref, gpost_ref, o_ref, acc_ref, *, tf):
    mixed = (_mm1(m0_ref[...], wo_ref[0:GROUP_W, :])
             + _mm1(m1_ref[...], wo_ref[GROUP_W:2 * GROUP_W, :])
             + _mm1(m2_ref[...], wo_ref[2 * GROUP_W:3 * GROUP_W, :])
             + _mm1(m3_ref[...], wo_ref[3 * GROUP_W:4 * GROUP_W, :]))
    x1 = x_ref[...] + _rms(mixed, gmix_ref[...])
    h = _rms(x1, gpre_ref[...]).astype(BF16)
    for c0 in range(0, D_FF, tf):
        a = jnp.maximum(_dot(h, wup_ref[:, c0:c0 + tf]), 0.0)
        part = _dot((a * a).astype(BF16), wdn_ref[c0:c0 + tf, :])
        if c0 == 0:
            acc_ref[...] = part
        else:
            acc_ref[...] += part
    o_ref[...] = x1 + _rms(acc_ref[...], gpost_ref[...])


def _post(mix_outs, x2, wo, gmix, gpre, wup, wdn, gpost, tm=512, tf=512):
    t = x2.shape[0]
    row = lambda w: pl.BlockSpec((tm, w), lambda i: (i, 0))
    single = lambda shape: pl.BlockSpec(shape, lambda i: (0, 0), pipeline_mode=pl.Buffered(1))
    return pl.pallas_call(
        functools.partial(_post_kernel, tf=tf),
        grid=(t // tm,),
        in_specs=[row(GROUP_W)] * 4 + [row(D_MODEL), single((D_MODEL, D_MODEL)),
                  _const_spec((1, D_MODEL)), _const_spec((1, D_MODEL)),
                  single((D_MODEL, D_FF)), single((D_FF, D_MODEL)), _const_spec((1, D_MODEL))],
        out_specs=row(D_MODEL),
        out_shape=jax.ShapeDtypeStruct((t, D_MODEL), F32),
        scratch_shapes=[pltpu.VMEM((tm, D_MODEL), F32)],
        compiler_params=_params("parallel"),
        name="out_proj_ffn",
    )(*mix_outs, x2, wo, gmix, gpre, wup, wdn, gpost)


def _s5_kernel(u_ref, wb_ref, wc_ref, pw_ref, d_ref, w1_ref, w2_ref, o_ref, st_ref, *, tb):
    @pl.when(pl.program_id(1) == 0)
    def _():
        st_ref[...] = jnp.zeros_like(st_ref)

    pieces = [slice(c * S5_SUB, (c + 1) * S5_SUB) for c in range(tb // S5_SUB)]
    us = [u_ref[rows, :] for rows in pieces]
    bus = [_mm1(u, wb_ref[...]) for u in us]
    cr, ci = st_ref[0:1, :], st_ref[1:2, :]
    for rows, u, bu in zip(pieces, us, bus):
        xrs, xis = [], []
        for j in range(S5_SUB // SUBLANES):
            xr = bu[j * SUBLANES:(j + 1) * SUBLANES, :S5_LANES]
            xi = bu[j * SUBLANES:(j + 1) * SUBLANES, S5_LANES:]
            for n, shift in enumerate((1, 2, 4)):
                pr = pw_ref[2 * n]
                pi = pw_ref[2 * n + 1]
                sr = pltpu.roll(xr, shift, axis=0)
                si = pltpu.roll(xi, shift, axis=0)
                xr, xi = xr + pr * sr - pi * si, xi + pr * si + pi * sr
            pr = pw_ref[6]
            pi = pw_ref[7]
            xr, xi = xr + pr * cr - pi * ci, xi + pr * ci + pi * cr
            cr, ci = xr[SUBLANES - 1:SUBLANES, :], xi[SUBLANES - 1:SUBLANES, :]
            xrs.append(xr)
            xis.append(xi)
        y = (_mm1(jnp.concatenate(xrs, axis=0), wc_ref[0:S5_LANES, :])
             + _mm1(jnp.concatenate(xis, axis=0), wc_ref[S5_LANES:, :]) + d_ref[...] * u)
        y = jax.nn.gelu(y).astype(BF16)
        o_ref[rows, :] = _dot(y, w1_ref[...]) * _sigmoid(_dot(y, w2_ref[...]))
    st_ref[0:1, :] = cr
    st_ref[1:2, :] = ci


def _s5(p32, bsz, seq, wb, wc, pw, d, w1, w2, tb=512):
    nb = seq // tb
    return pl.pallas_call(
        functools.partial(_s5_kernel, tb=tb),
        grid=(bsz, nb),
        in_specs=[pl.BlockSpec((tb, GROUP_W), lambda b, i: (b * nb + i, P32_S5)),
                  _const_spec((GROUP_W, 2 * S5_LANES)),
                  _const_spec((2 * S5_LANES, GROUP_W)),
                  _const_spec((8, SUBLANES, S5_LANES)),
                  _const_spec((1, GROUP_W)),
                  _const_spec((GROUP_W, GROUP_W)),
                  _const_spec((GROUP_W, GROUP_W))],
        out_specs=pl.BlockSpec((tb, GROUP_W), lambda b, i: (b * nb + i, 0)),
        out_shape=jax.ShapeDtypeStruct((bsz * seq, GROUP_W), F32),
        scratch_shapes=[pltpu.VMEM((SUBLANES, S5_LANES), F32)],
        compiler_params=_params("parallel", "arbitrary"),
        name="s5",
    )(p32, wb, wc, pw, d, w1, w2)


def _s5_tables(a_re, a_im, log_dt, b_re, b_im, c_re, c_im):
    dt = jnp.exp(log_dt)[:, None]
    mag = jnp.exp(a_re * dt)
    ab_re = mag * jnp.cos(a_im * dt)
    ab_im = mag * jnp.sin(a_im * dt)
    den = a_re * a_re + a_im * a_im
    num_re = ab_re - 1.0
    zoh_re = (num_re * a_re + ab_im * a_im) / den
    zoh_im = (ab_im * a_re - num_re * a_im) / den
    bb_re = zoh_re[..., None] * b_re - zoh_im[..., None] * b_im
    bb_im = zoh_re[..., None] * b_im + zoh_im[..., None] * b_re
    eye = jnp.eye(S5_GROUPS, dtype=F32)
    blk_in = lambda m: jnp.einsum("gpc,gh->gchp", m, eye).reshape(GROUP_W, S5_LANES)
    blk_out = lambda m: jnp.einsum("gcp,gh->gphc", m, eye).reshape(S5_LANES, GROUP_W)
    wb = jnp.concatenate([blk_in(bb_re), blk_in(bb_im)], axis=1).astype(BF16)
    wc = jnp.concatenate([blk_out(c_re), -blk_out(c_im)], axis=0).astype(BF16)

    def power(n):
        m = jnp.exp(n * (a_re * dt)[None])
        ang = n * (a_im * dt)[None]
        return ((m * jnp.cos(ang)).reshape(-1, S5_LANES), (m * jnp.sin(ang)).reshape(-1, S5_LANES))

    rows = jnp.arange(SUBLANES, dtype=F32)[:, None, None]
    tabs = []
    for shift in (1, 2, 4):
        pr, pi = power(jnp.full_like(rows, float(shift)))
        keep = (jnp.arange(SUBLANES) >= shift)[:, None]
        tabs += [jnp.where(keep, pr, 0.0), jnp.where(keep, pi, 0.0)]
    pr, pi = power(rows + 1.0)
    tabs += [pr, pi]
    return wb, wc, jnp.stack(tabs)


def _rope(t, c, s_up, s_dn):
    return (t * c + pltpu.roll(t, HEAD_DIM // 2, axis=1) * s_up
            + pltpu.roll(t, GROUP_W - HEAD_DIM // 2, axis=1) * s_dn)


def _ret_kernel(q_ref, k_ref, v_ref, g_ref, cos_ref, sup_ref, sdn_ref, intra_ref, qd_ref, kd_ref,
                cd_ref, hm_ref, bd_ref, seg_ref, o_ref, st_ref, *, tb):
    @pl.when(pl.program_id(1) == 0)
    def _():
        st_ref[...] = jnp.zeros_like(st_ref)

    for c in range(tb // RET_CHUNK):
        rows = slice(c * RET_CHUNK, (c + 1) * RET_CHUNK)
        cos = cos_ref[rows, :]
        sup = sup_ref[rows, :]
        sdn = sdn_ref[rows, :]
        q = _rope(q_ref[rows, :], cos, sup, sdn)
        k = _rope(k_ref[rows, :], cos, sup, sdn) * HEAD_DIM ** -0.5
        v = v_ref[rows, :]
        kb = k.astype(BF16)
        vb = v.astype(BF16)
        state = st_ref[...]
        o = _mm1(q * qd_ref[...], state)
        for h in range(N_HEADS):
            hm = hm_ref[h:h + 1, :]
            scores = _dot((q * hm).astype(BF16), kb, NT) * intra_ref[h]
            o = o + hm * _dot(scores.astype(BF16), vb)
        st_ref[...] = state * cd_ref[...] + bd_ref[...] * _dot((k * kd_ref[...]).astype(BF16), vb, TN)
        ms = _mm2(o * o, seg_ref[...]) * (1.0 / HEAD_DIM)
        o = o * lax.rsqrt(ms + NORM_EPS)
        g = g_ref[rows, :]
        o_ref[rows, :] = g * _sigmoid(g) * o


def _retention(p32, bsz, seq, tabs, consts, tb=512):
    nb = seq // tb
    col = lambda j: pl.BlockSpec((tb, GROUP_W), lambda b, i: (b * nb + i, j))
    pos = pl.BlockSpec((tb, GROUP_W), lambda b, i: (i, 0))
    cos, sup, sdn, intra, qd, kd, cd = tabs
    return pl.pallas_call(
        functools.partial(_ret_kernel, tb=tb),
        grid=(bsz, nb),
        in_specs=[col(P32_RET), col(P32_RET + 1), col(P32_RET + 2), col(P32_RET + 3), pos, pos, pos,
                  _const_spec((N_HEADS, RET_CHUNK, RET_CHUNK)),
                  _const_spec((RET_CHUNK, GROUP_W)), _const_spec((RET_CHUNK, GROUP_W)),
                  _const_spec((GROUP_W, GROUP_W)), _const_spec((N_HEADS, GROUP_W)),
                  _const_spec((GROUP_W, GROUP_W)), _const_spec((GROUP_W, GROUP_W))],
        out_specs=pl.BlockSpec((tb, GROUP_W), lambda b, i: (b * nb + i, 0)),
        out_shape=jax.ShapeDtypeStruct((bsz * seq, GROUP_W), F32),
        scratch_shapes=[pltpu.VMEM((GROUP_W, GROUP_W), F32)],
        compiler_params=_params("parallel", "arbitrary"),
        name="retention",
    )(p32, p32, p32, p32, cos, sup, sdn, intra, qd, kd, cd, consts["hm"], consts["bd"], consts["seg"])


def _retention_tables(seq):
    inv_freq = ROPE_BASE ** (-jnp.arange(0, HEAD_DIM, 2, dtype=F32) / HEAD_DIM)
    ang = jnp.arange(seq, dtype=F32)[:, None] * inv_freq[None, :]
    cos, sin = jnp.cos(ang), jnp.sin(ang)
    zero = jnp.zeros_like(sin)
    cos_t = jnp.tile(jnp.concatenate([cos, cos], axis=1), (1, N_HEADS))
    sup_t = jnp.tile(jnp.concatenate([zero, sin], axis=1), (1, N_HEADS))
    sdn_t = jnp.tile(jnp.concatenate([-sin, zero], axis=1), (1, N_HEADS))
    log_gamma = jnp.log1p(-jnp.exp2(-5.0 - jnp.arange(N_HEADS, dtype=F32)))
    idx = jnp.arange(RET_CHUNK, dtype=F32)
    rel = idx[:, None] - idx[None, :]
    intra = jnp.where(rel >= 0, jnp.exp(log_gamma[:, None, None] * jnp.maximum(rel, 0.0)), 0.0)
    lanes = jnp.repeat(log_gamma, HEAD_DIM)[None, :]
    qd = jnp.exp(lanes * (idx + 1.0)[:, None])
    kd = jnp.exp(lanes * (RET_CHUNK - 1.0 - idx)[:, None])
    cd = jnp.broadcast_to(jnp.exp(lanes * RET_CHUNK), (GROUP_W, GROUP_W))
    return cos_t, sup_t, sdn_t, intra, qd, kd, cd


def _sb_kernel(q_ref, k_ref, vt_ref, hm4_ref, cumt_ref, neg1_ref, trib_ref, trin_ref, o_ref, acc_ref):
    qb = pl.program_id(1)
    nsub = SB_Q // SB_BLOCK
    q = q_ref[...] * HEAD_DIM ** -0.5
    q4 = _tile4(q) * hm4_ref[...]
    cumt = cumt_ref[...]
    neg1 = neg1_ref[...]
    acc_ref[...] = jnp.zeros_like(acc_ref)
    heads = range(N_HEADS)
    dims = lambda h: slice(h * HEAD_DIM, (h + 1) * HEAD_DIM)

    def first_cols(x, f):
        head = f(x[:, :SB_BLOCK])
        return head if x.shape[1] == SB_BLOCK else jnp.concatenate([head, x[:, SB_BLOCK:]], axis=1)

    def add_blocks(tiles, runs):
        runs = list(runs)
        logits = []
        for blk, lo, _ in tiles:
            s0 = pl.multiple_of(blk * SB_BLOCK, SB_BLOCK)
            logits += [_dot(k_ref[pl.ds(s0, SB_BLOCK), :], q4[h * SB_Q + lo:(h + 1) * SB_Q, :], NT) for h in heads]
        pending = []
        for t, (blk, lo, diagonal) in enumerate(tiles):
            for h in heads:
                zt = logits[t * N_HEADS + h]
                sp = jnp.maximum(zt, 0.0) + jnp.log(1.0 + jnp.exp(-jnp.abs(zt)))
                log_w = zt - sp
                spb = sp.astype(BF16)
                if diagonal:
                    spb = first_cols(spb, lambda x: x * trib_ref[...])
                    log_w = first_cols(log_w, lambda x: x + trin_ref[...])
                run = runs[h][:, lo:]
                log_w = log_w + _dot(cumt, spb) + run
                run = run + _dot(neg1, spb)[0:1, :]
                runs[h] = run if lo == 0 else jnp.concatenate([runs[h][:, :lo], run], axis=1)
                pending.append((blk, h, lo, log_w))
        for blk, h, lo, log_w in pending:
            acc_ref[dims(h), lo:] += _dot(vt_ref[blk, dims(h), :], jnp.exp(log_w).astype(BF16))
        return tuple(runs)

    runs = tuple(jnp.zeros((1, SB_Q), F32) for _ in heads)
    top = qb * nsub
    for m in range(nsub - 1, 0, -2):
        runs = add_blocks([(top + m, m * SB_BLOCK, True), (top + m - 1, (m - 1) * SB_BLOCK, True)], runs)

    def live(state):
        return jnp.logical_and(state[0] < qb * (nsub // 2), state[1] > SB_DEAD)

    def pair(state):
        jj, runs = state[0], state[2:]
        blk = top - 1 - 2 * jj
        runs = add_blocks([(blk, 0, False), (blk - 1, 0, False)], runs)
        slowest = jnp.max(jnp.maximum(jnp.maximum(runs[0], runs[1]), jnp.maximum(runs[2], runs[3])))
        return (jj + 1, slowest) + runs

    lax.while_loop(live, pair, (jnp.int32(0), jnp.float32(0.0)) + runs)
    o_ref[...] = acc_ref[...].T


def _stick_breaking(p16, bsz, seq, consts):
    nq = seq // SB_Q
    nk = seq // SB_BLOCK
    vt = p16[:, 2 * GROUP_W:].reshape(bsz * nk, SB_BLOCK, GROUP_W).swapaxes(1, 2)
    return pl.pallas_call(
        _sb_kernel,
        grid=(bsz, nq),
        in_specs=[pl.BlockSpec((SB_Q, GROUP_W), lambda b, i: (b * nq + i, 0)),
                  pl.BlockSpec((seq, GROUP_W), lambda b, i: (b, 1)),
                  pl.BlockSpec((nk, GROUP_W, SB_BLOCK), lambda b, i: (b, 0, 0)),
                  _const_spec((N_HEADS * SB_Q, GROUP_W)),
                  _const_spec((SB_BLOCK, SB_BLOCK)),
                  _const_spec((SUBLANES, SB_BLOCK)),
                  _const_spec((SB_BLOCK, SB_BLOCK)),
                  _const_spec((SB_BLOCK, SB_BLOCK))],
        out_specs=pl.BlockSpec((SB_Q, GROUP_W), lambda b, i: (b * nq + i, 0)),
        out_shape=jax.ShapeDtypeStruct((bsz * seq, GROUP_W), F32),
        scratch_shapes=[pltpu.VMEM((GROUP_W, SB_Q), F32)],
        compiler_params=_params("parallel", "arbitrary"),
        name="stick_breaking",
    )(p16, p16, vt, consts["sb_hm4"], consts["sb_cumt"], consts["sb_neg1"], consts["sb_trib"], consts["sb_trin"])


def _tile4(x):
    return jnp.concatenate([x, x, x, x], axis=0)


def _fold4(x):
    n = x.shape[0] // N_HEADS
    return x[0:n] + x[n:2 * n] + x[2 * n:3 * n] + x[3 * n:4 * n]


def _shifted(cur, prev_ref, first_row):
    sh = jnp.where(first_row, prev_ref[0:1, :], pltpu.roll(cur, 1, axis=0))
    prev_ref[0:1, :] = cur[cur.shape[0] - 1:, :]
    return sh


def _rwkv_kernel(*refs, tb, has_vres):
    if has_vres:
        (cols_ref, vres_ref, vfirst_ref, mu_ref, vmu_ref, w0_ref, w2_ref, a0_ref, a2_ref, g2_ref,
         v0_ref, v2_ref, kk_ref, ka_ref, rk_ref, lnw_ref, lnb_ref,
         hm4_ref, sl_ref, il_ref, eye_ref, ltri_ref, seg_ref,
         o_ref, z_ref, prev_ref, prevv_ref) = refs
    else:
        (cols_ref, mu_ref, w0_ref, w2_ref, a0_ref, a2_ref, g2_ref,
         kk_ref, ka_ref, rk_ref, lnw_ref, lnb_ref,
         hm4_ref, sl_ref, il_ref, eye_ref, ltri_ref, seg_ref,
         o_ref, vout_ref, z_ref, prev_ref) = refs

    @pl.when(pl.program_id(1) == 0)
    def _():
        z_ref[...] = jnp.zeros_like(z_ref)
        prev_ref[...] = jnp.zeros_like(prev_ref)
        if has_vres:
            prevv_ref[...] = jnp.zeros_like(prevv_ref)

    first_row = lax.broadcasted_iota(jnp.int32, (tb, 1), 0) == 0
    cols = cols_ref[...]
    xs = cols + (_shifted(cols, prev_ref, first_row) - cols) * mu_ref[...]
    r = xs[:, 0:GROUP_W]
    k = xs[:, GROUP_W:2 * GROUP_W]
    v = xs[:, 2 * GROUP_W:3 * GROUP_W]
    wa = xs[:, 3 * GROUP_W:3 * GROUP_W + 128]
    gd = xs[:, 3 * GROUP_W + 128:4 * GROUP_W]
    w_log = -_softplus(-(w0_ref[...] + _mm1(jnp.tanh(wa), w2_ref[...]))) - 0.5
    lw = -jnp.exp(w_log)
    a = _sigmoid(a0_ref[...] + _mm1(wa, a2_ref[...]))
    g = _mm1(_sigmoid(gd), g2_ref[...])
    if has_vres:
        vr = vres_ref[...]
        vx = vr + (_shifted(vr, prevv_ref, first_row) - vr) * vmu_ref[...]
        v = v + (vfirst_ref[...] - v) * _sigmoid(v0_ref[...] + _mm1(vx, v2_ref[...]))
    else:
        vout_ref[...] = v
    seg = seg_ref[...]
    kk = k * kk_ref[...]
    kk = kk * lax.rsqrt(jnp.maximum(_mm2(kk * kk, seg), 1e-12))
    k = k * (1.0 + (a - 1.0) * ka_ref[...])
    aa = -kk
    bb = kk * a

    hm4 = hm4_ref[...]
    sl = sl_ref[...]
    il = il_ref[...]
    eye = eye_ref[...]

    w256 = GROUP_W
    ltri = ltri_ref[...]
    each = lambda f, *lists: [f(*xs) for xs in zip(*lists)]

    def chunk_maps(group):
        take = lambda x: [x[ci * RW_CHUNK:(ci + 1) * RW_CHUNK, :] for ci in group]
        rc, kc, vc, ac, bc, lwc = take(r), take(k), take(v), take(aa), take(bb), take(lw)
        c = each(lambda x: _mm2l(ltri, x), lwc)
        c_end = each(lambda x: x[RW_CHUNK - 1:, :], c)
        at4 = each(lambda a_, c_, l_: _tile4(a_ * jnp.exp(c_ - l_)) * hm4, ac, c, lwc)
        rt4 = each(lambda r_, c_: _tile4(r_ * jnp.exp(c_)) * hm4, rc, c)
        at4b = each(lambda x: x.astype(BF16), at4)
        be4b = each(lambda b_, c_, e_: (_tile4(b_ * jnp.exp(e_ - c_)) * hm4).astype(BF16), bc, c, c_end)
        ke4b = each(lambda k_, c_, e_: (_tile4(k_ * jnp.exp(e_ - c_)) * hm4).astype(BF16), kc, c, c_end)
        v4b = each(lambda v_: (_tile4(v_) * hm4).astype(BF16), vc)
        lhs = each(lambda a_, r_: jnp.concatenate([a_, r_.astype(BF16)], axis=0), at4b, rt4)
        rhs = each(lambda b_, k_, c_: jnp.concatenate([_tile4((b_ * jnp.exp(-c_)).astype(BF16)),
                                                        _tile4((k_ * jnp.exp(-c_)).astype(BF16))], axis=0),
                   bc, kc, c)
        nn = each(lambda l_, r_: _dot(l_, r_, NT), lhs, rhs)
        n_ab = each(lambda x: x[:w256, :w256] * sl, nn)
        n_ak = each(lambda x: (x[:w256, w256:] * sl).astype(BF16), nn)
        n_rb = each(lambda x: (x[w256:, :w256] * il).astype(BF16), nn)
        n_rk = each(lambda x: (x[w256:, w256:] * il).astype(BF16), nn)
        yield None
        inv = each(lambda x: eye + x, n_ab)
        pw = each(lambda x: x.astype(BF16), n_ab)
        for _ in range(int(math.log2(RW_CHUNK)) - 1):
            pw = each(lambda x: _dot(x, x).astype(BF16), pw)
            inv = each(lambda i_, p_: i_ + _dot(p_, i_.astype(BF16)), inv, pw)
            yield None
        akv = each(lambda n_, v_: _dot(n_, v_).astype(BF16), n_ak, v4b)
        ugb = each(lambda i_, x_, a_: _dot(i_.astype(BF16), jnp.concatenate([x_, a_], axis=1)).astype(BF16),
                   inv, akv, at4b)
        yield None
        yh = each(_dot, n_rb, ugb)
        y0 = each(lambda y_, n_, v_: y_[:, :w256] + _dot(n_, v_), yh, n_rk, v4b)
        hmat = each(lambda r_, y_: (r_ + y_[:, w256:]).astype(BF16), rt4, yh)
        yield None
        qp = each(lambda b_, u_: _dot(b_, u_, TN), be4b, ugb)
        qm = each(lambda q_, k_, v_: q_[:, :w256] + _dot(k_, v_, TN), qp, ke4b, v4b)
        p = each(lambda e_, q_: (eye * jnp.exp(e_) + q_[:, w256:]).astype(BF16), c_end, qp)
        yield list(zip(y0, hmat, p, qm))

    z = z_ref[...]
    ys = []

    def state_step(chunk_map):
        nonlocal z
        y0, hmat, p, qm = chunk_map
        zb = z.astype(BF16)
        ys.append(_fold4(y0 + _dot(hmat, zb)))
        z = _dot(p, zb) + qm

    n_chunks = tb // RW_CHUNK
    waiting = []
    for first in range(0, n_chunks, RW_GROUP):
        for stage, maps in enumerate(chunk_maps(range(first, min(first + RW_GROUP, n_chunks)))):
            if waiting and stage % 2 == 1:
                state_step(waiting.pop(0))
        for chunk_map in waiting:
            state_step(chunk_map)
        waiting = maps
    for chunk_map in waiting:
        state_step(chunk_map)
    z_ref[...] = z

    y = jnp.concatenate(ys, axis=0)
    inv_n = 1.0 / HEAD_DIM
    mean = _mm2(y, seg) * inv_n
    yc = y - mean
    var = _mm2(yc * yc, seg) * inv_n
    y = yc * lax.rsqrt(var + RWKV_GN_EPS) * lnw_ref[...] + lnb_ref[...]
    bonus = _mm2(r * k * rk_ref[...], seg) * v
    o_ref[...] = (y + bonus) * g


def _rwkv(p32, vfirst, bsz, seq, lw, consts, has_vres, tb=512):
    nb = seq // tb
    rows = lambda w, j: pl.BlockSpec((tb, w), lambda b, i: (b * nb + i, j))
    vec = _const_spec((1, GROUP_W))
    mat = lambda n: _const_spec((n, GROUP_W))
    big = _const_spec((GROUP_W, GROUP_W))
    cnames = ("hm4", "sl", "il", "eye", "ltri", "seg")
    cspecs = [big, big, big, big, _const_spec((RW_CHUNK, RW_CHUNK)), big]
    cargs = [consts[n] for n in cnames]
    cols_spec = pl.BlockSpec((tb, 4 * GROUP_W), lambda b, i: (b * nb + i, 0))
    out_spec = pl.BlockSpec((tb, GROUP_W), lambda b, i: (b * nb + i, 0))
    out_sds = jax.ShapeDtypeStruct((bsz * seq, GROUP_W), F32)
    scratch = [pltpu.VMEM((GROUP_W, GROUP_W), F32), pltpu.VMEM((SUBLANES, 4 * GROUP_W), F32)]
    if has_vres:
        in_specs = ([cols_spec, rows(GROUP_W, P32_VRES), rows(GROUP_W, 0), _const_spec((1, 4 * GROUP_W)), vec,
                     vec, mat(128), vec, mat(128), mat(RWKV_LORA_G), vec, mat(GROUP_W), vec, vec, vec, vec, vec]
                    + cspecs)
        args = [p32, p32, vfirst, lw["mu"], lw["vmu"], lw["w0"], lw["w2"], lw["a0"], lw["a2"],
                lw["g2"], lw["v0"], lw["v2"], lw["k_k"], lw["k_a"], lw["r_k"], lw["ln_w"], lw["ln_b"]] + cargs
        out_specs, out_shape = out_spec, out_sds
        scratch = scratch + [pltpu.VMEM((SUBLANES, GROUP_W), F32)]
    else:
        in_specs = ([cols_spec, _const_spec((1, 4 * GROUP_W)), vec, mat(128), vec, mat(128), mat(RWKV_LORA_G),
                     vec, vec, vec, vec, vec] + cspecs)
        args = [p32, lw["mu"], lw["w0"], lw["w2"], lw["a0"], lw["a2"], lw["g2"],
                lw["k_k"], lw["k_a"], lw["r_k"], lw["ln_w"], lw["ln_b"]] + cargs
        out_specs, out_shape = [out_spec, out_spec], [out_sds, out_sds]
    return pl.pallas_call(
        functools.partial(_rwkv_kernel, tb=tb, has_vres=has_vres),
        grid=(bsz, nb),
        in_specs=in_specs,
        out_specs=out_specs,
        out_shape=out_shape,
        scratch_shapes=scratch,
        compiler_params=_params("parallel", "arbitrary"),
        name="rwkv7",
    )(*args)


def _constants():
    lane_head = jnp.arange(GROUP_W) // HEAD_DIM
    hm = (lane_head[None, :] == jnp.arange(N_HEADS)[:, None]).astype(F32)
    bd = (lane_head[:, None] == lane_head[None, :])
    seg = bd.astype(BF16)
    pos = jnp.arange(GROUP_W) % RW_CHUNK
    hm4 = bd.astype(F32)
    sl = (bd & (pos[None, :] < pos[:, None])).astype(F32)
    il = (bd & (pos[None, :] <= pos[:, None])).astype(F32)
    eye = jnp.eye(GROUP_W, dtype=F32)
    t = jnp.arange(RW_CHUNK)
    ltri = (t[None, :] <= t[:, None]).astype(BF16)
    j = jnp.arange(SB_BLOCK)
    sb_cumt = -(j[None, :] > j[:, None]).astype(BF16)
    sb_neg1 = -jnp.ones((SUBLANES, SB_BLOCK), BF16)
    sb_hm4 = jnp.repeat(hm, SB_Q, axis=0).astype(BF16)
    visible = j[:, None] < j[None, :]
    sb_trib = visible.astype(BF16)
    sb_trin = jnp.where(visible, 0.0, SB_MASKED).astype(F32)
    return dict(hm=hm, bd=bd.astype(F32), seg=seg, hm4=hm4, sl=sl, il=il, eye=eye, ltri=ltri,
                sb_cumt=sb_cumt, sb_neg1=sb_neg1, sb_hm4=sb_hm4, sb_trib=sb_trib, sb_trin=sb_trin)


def kernel(x, norm_mix_pre, norm_mix_post, norm_ffn_pre, norm_ffn_post, w_in_first, w_in_rest, w_out,
           s5_a_re, s5_a_im, s5_log_dt, s5_b_re, s5_b_im, s5_c_re, s5_c_im, s5_d, s5_glu_w1, s5_glu_w2,
           rw_mu, rw_vres_mu, rw_w0, rw_w2, rw_a0, rw_a2, rw_g2, rw_v0, rw_v2,
           rw_k_k, rw_k_a, rw_r_k, rw_ln_w, rw_ln_b, w_up, w_down):
    bsz, seq, _ = x.shape
    consts = _constants()
    ret_tabs = _retention_tables(seq)
    x2 = x.reshape(bsz * seq, D_MODEL)
    row = lambda vec: vec.reshape(1, -1)
    zpad = lambda m, n: jnp.pad(m, ((0, n - m.shape[0]), (0, 0)))
    v_first = None
    for l in range(DEPTH):
        w_in = w_in_first if l == 0 else w_in_rest[l - 1]
        parts = [w_in[:, OFF_RW:N_IN0], w_in[:, OFF_RET:OFF_SB], w_in[:, :OFF_RET]]
        if l > 0:
            parts.append(jnp.pad(w_in[:, N_IN0:], ((0, 0), (0, GROUP_W - RWKV_LORA_V))))
        n32 = sum(p.shape[1] for p in parts)
        w_cat = jnp.concatenate(parts + [w_in[:, OFF_SB:OFF_RW]], axis=1).astype(BF16)
        p32, p16 = _in_proj(x2, row(norm_mix_pre[l]), w_cat, n32, 3 * GROUP_W)

        wb, wc, pw = _s5_tables(s5_a_re[l], s5_a_im[l], s5_log_dt[l], s5_b_re[l], s5_b_im[l],
                                s5_c_re[l], s5_c_im[l])
        out_s5 = _s5(p32, bsz, seq, wb, wc, pw, row(s5_d[l]),
                     s5_glu_w1[l].astype(BF16), s5_glu_w2[l].astype(BF16))
        out_ret = _retention(p32, bsz, seq, ret_tabs, consts)
        out_sb = _stick_breaking(p16, bsz, seq, consts)

        lw = dict(
            mu=row(rw_mu[l]), w0=row(rw_w0[l]), a0=row(rw_a0[l]),
            w2=zpad(rw_w2[l], 128).astype(BF16),
            a2=jnp.pad(rw_a2[l], ((RWKV_LORA_W, 0), (0, 0))).astype(BF16),
            g2=rw_g2[l].astype(BF16),
            k_k=row(rw_k_k[l]), k_a=row(rw_k_a[l]), r_k=row(rw_r_k[l]),
            ln_w=row(rw_ln_w[l]), ln_b=row(rw_ln_b[l]))
        if l == 0:
            out_rw, v_first = _rwkv(p32, None, bsz, seq, lw, consts, False)
        else:
            lw.update(vmu=row(jnp.pad(rw_vres_mu[l - 1], (0, GROUP_W - RWKV_LORA_V))),
                      v0=row(rw_v0[l - 1]), v2=zpad(rw_v2[l - 1], GROUP_W).astype(BF16))
            out_rw = _rwkv(p32, v_first, bsz, seq, lw, consts, True)

        x2 = _post((out_s5, out_ret, out_rw, out_sb), x2, w_out[l].astype(BF16), row(norm_mix_post[l]),
                   row(norm_ffn_pre[l]), w_up[l].astype(BF16), w_down[l].astype(BF16), row(norm_ffn_post[l]))
    return x2.reshape(bsz, seq, D_MODEL)
```

```python
import functools
import math

import jax
import jax.numpy as jnp
from jax import lax
from jax.experimental import pallas as pl
from jax.experimental.pallas import tpu as pltpu

F32 = jnp.float32
BF16 = jnp.bfloat16

D_MODEL = 1024
DEPTH = 4
GROUP_W = 256
HEAD_DIM = 64
N_HEADS = 4
S5_GROUP_CH = 16
S5_GROUPS = 16
S5_STATE = 64
S5_LANES = S5_GROUPS * S5_STATE
S5_SUB = 128
RET_CHUNK = 128
ROPE_BASE = 10000.0
RWKV_LORA_W = 64
RWKV_LORA_A = 64
RWKV_LORA_V = 32
RWKV_LORA_G = 128
RWKV_GN_EPS = 64e-5
RW_CHUNK = 64
RW_GROUP = 4
SB_BLOCK = 128
SB_Q = 512
SB_DEAD = -110.0
SB_MASKED = -1e30
D_FF = 4 * D_MODEL
NORM_EPS = 1e-6

OFF_RET = GROUP_W
OFF_SB = OFF_RET + 4 * GROUP_W
OFF_RW = OFF_SB + 3 * GROUP_W
N_IN0 = OFF_RW + 4 * GROUP_W

P32_RET = 4
P32_S5 = 8
P32_VRES = 9

SUBLANES = 8
VMEM_LIMIT = 56 * 1024 * 1024

NN = (((1,), (0,)), ((), ()))
NT = (((1,), (1,)), ((), ()))
TN = (((0,), (0,)), ((), ()))


def _dot(a, b, dims=NN):
    return lax.dot_general(a, b, dims, preferred_element_type=F32)


def _split(x):
    hi = x.astype(BF16)
    lo = (x - hi.astype(F32)).astype(BF16)
    return hi, lo


def _mm2(a, b_exact, dims=NN):
    ah, al = _split(a)
    return _dot(ah, b_exact, dims) + _dot(al, b_exact, dims)


def _mm2l(a_exact, b, dims=NN):
    bh, bl = _split(b)
    return _dot(a_exact, bh, dims) + _dot(a_exact, bl, dims)


def _mm1(a, b, dims=NN):
    return _dot(a.astype(BF16), b.astype(BF16), dims)


def _softplus(x):
    return jnp.maximum(x, 0.0) + jnp.log(1.0 + jnp.exp(-jnp.abs(x)))


def _sigmoid(x):
    return 1.0 / (1.0 + jnp.exp(-x))


def _rms(x, gain):
    return x * lax.rsqrt(jnp.mean(x * x, axis=-1, keepdims=True) + NORM_EPS) * gain


def _params(*sem):
    return pltpu.CompilerParams(dimension_semantics=sem, vmem_limit_bytes=VMEM_LIMIT)


def _const_spec(shape):
    zeros = (0,) * len(shape)
    return pl.BlockSpec(shape, lambda *_: zeros)


def _in_proj_kernel(x_ref, g_ref, w_ref, o32_ref, o16_ref, vt_ref, *, n32, n16, tn):
    h = _rms(x_ref[...], g_ref[...]).astype(BF16)
    for c0 in range(0, n32, tn):
        o32_ref[:, c0:c0 + tn] = _dot(h, w_ref[:, c0:c0 + tn])
    for c0 in range(0, n16, tn):
        o16_ref[:, c0:c0 + tn] = _dot(h, w_ref[:, n32 + c0:n32 + c0 + tn]).astype(BF16)
    v = _dot(h, w_ref[:, n32 + n16:])
    for j in range(vt_ref.shape[0]):
        vt_ref[j] = v[j * SB_BLOCK:(j + 1) * SB_BLOCK, :].T.astype(BF16)


def _in_proj(x2, gain, w, n32, n16, tm=512, tn=256):
    t = x2.shape[0]
    kern = functools.partial(_in_proj_kernel, n32=n32, n16=n16, tn=tn)
    return pl.pallas_call(
        kern,
        grid=(t // tm,),
        in_specs=[pl.BlockSpec((tm, D_MODEL), lambda i: (i, 0)),
                  _const_spec((1, D_MODEL)),
                  _const_spec((D_MODEL, n32 + n16 + GROUP_W))],
        out_specs=[pl.BlockSpec((tm, n32), lambda i: (i, 0)),
                   pl.BlockSpec((tm, n16), lambda i: (i, 0)),
                   pl.BlockSpec((tm // SB_BLOCK, GROUP_W, SB_BLOCK), lambda i: (i, 0, 0))],
        out_shape=[jax.ShapeDtypeStruct((t, n32), F32),
                   jax.ShapeDtypeStruct((t, n16), BF16),
                   jax.ShapeDtypeStruct((t // SB_BLOCK, GROUP_W, SB_BLOCK), BF16)],
        compiler_params=_params("parallel"),
        name="in_proj",
    )(x2, gain, w)


def _post_kernel(m0_ref, m1_ref, m2_ref, m3_ref, x_ref, wo_ref, gmix_ref, gpre_ref,
                 wup_ref, wdn_ref, gpost_ref, o_ref, acc_ref, *, tf):
    mixed = (_mm1(m0_ref[...], wo_ref[0:GROUP_W, :])
             + _mm1(m1_ref[...], wo_ref[GROUP_W:2 * GROUP_W, :])
             + _mm1(m2_ref[...], wo_ref[2 * GROUP_W:3 * GROUP_W, :])
             + _mm1(m3_ref[...], wo_ref[3 * GROUP_W:4 * GROUP_W, :]))
    x1 = x_ref[...] + _rms(mixed, gmix_ref[...])
    h = _rms(x1, gpre_ref[...]).astype(BF16)
    for c0 in range(0, D_FF, tf):
        a = jnp.maximum(_dot(h, wup_ref[:, c0:c0 + tf]), 0.0)
        part = _dot((a * a).astype(BF16), wdn_ref[c0:c0 + tf, :])
        if c0 == 0:
            acc_ref[...] = part
        else:
            acc_ref[...] += part
    o_ref[...] = x1 + _rms(acc_ref[...], gpost_ref[...])


def _post(mix_outs, x2, wo, gmix, gpre, wup, wdn, gpost, tm=512, tf=512):
    t = x2.shape[0]
    row = lambda w: pl.BlockSpec((tm, w), lambda i: (i, 0))
    single = lambda shape: pl.BlockSpec(shape, lambda i: (0, 0), pipeline_mode=pl.Buffered(1))
    return pl.pallas_call(
        functools.partial(_post_kernel, tf=tf),
        grid=(t // tm,),
        in_specs=[row(GROUP_W)] * 4 + [row(D_MODEL), single((D_MODEL, D_MODEL)),
                  _const_spec((1, D_MODEL)), _const_spec((1, D_MODEL)),
                  single((D_MODEL, D_FF)), single((D_FF, D_MODEL)), _const_spec((1, D_MODEL))],
        out_specs=row(D_MODEL),
        out_shape=jax.ShapeDtypeStruct((t, D_MODEL), F32),
        scratch_shapes=[pltpu.VMEM((tm, D_MODEL), F32)],
        compiler_params=_params("parallel"),
        name="out_proj_ffn",
    )(*mix_outs, x2, wo, gmix, gpre, wup, wdn, gpost)


def _s5_kernel(u_ref, wb_ref, wc_ref, pw_ref, d_ref, w1_ref, w2_ref, o_ref, st_ref, *, tb):
    @pl.when(pl.program_id(1) == 0)
    def _():
        st_ref[...] = jnp.zeros_like(st_ref)

    pieces = [slice(c * S5_SUB, (c + 1) * S5_SUB) for c in range(tb // S5_SUB)]
    us = [u_ref[rows, :] for rows in pieces]
    bus = [_mm1(u, wb_ref[...]) for u in us]
    cr, ci = st_ref[0:1, :], st_ref[1:2, :]
    for rows, u, bu in zip(pieces, us, bus):
        xrs, xis = [], []
        for j in range(S5_SUB // SUBLANES):
            xr = bu[j * SUBLANES:(j + 1) * SUBLANES, :S5_LANES]
            xi = bu[j * SUBLANES:(j + 1) * SUBLANES, S5_LANES:]
            for n, shift in enumerate((1, 2, 4)):
                pr = pw_ref[2 * n]
                pi = pw_ref[2 * n + 1]
                sr = pltpu.roll(xr, shift, axis=0)
                si = pltpu.roll(xi, shift, axis=0)
                xr, xi = xr + pr * sr - pi * si, xi + pr * si + pi * sr
            pr = pw_ref[6]
            pi = pw_ref[7]
            xr, xi = xr + pr * cr - pi * ci, xi + pr * ci + pi * cr
            cr, ci = xr[SUBLANES - 1:SUBLANES, :], xi[SUBLANES - 1:SUBLANES, :]
            xrs.append(xr)
            xis.append(xi)
        y = (_mm1(jnp.concatenate(xrs, axis=0), wc_ref[0:S5_LANES, :])
             + _mm1(jnp.concatenate(xis, axis=0), wc_ref[S5_LANES:, :]) + d_ref[...] * u)
        y = jax.nn.gelu(y).astype(BF16)
        o_ref[rows, :] = _dot(y, w1_ref[...]) * _sigmoid(_dot(y, w2_ref[...]))
    st_ref[0:1, :] = cr
    st_ref[1:2, :] = ci


def _s5(p32, bsz, seq, wb, wc, pw, d, w1, w2, tb=512):
    nb = seq // tb
    return pl.pallas_call(
        functools.partial(_s5_kernel, tb=tb),
        grid=(bsz, nb),
        in_specs=[pl.BlockSpec((tb, GROUP_W), lambda b, i: (b * nb + i, P32_S5)),
                  _const_spec((GROUP_W, 2 * S5_LANES)),
                  _const_spec((2 * S5_LANES, GROUP_W)),
                  _const_spec((8, SUBLANES, S5_LANES)),
                  _const_spec((1, GROUP_W)),
                  _const_spec((GROUP_W, GROUP_W)),
                  _const_spec((GROUP_W, GROUP_W))],
        out_specs=pl.BlockSpec((tb, GROUP_W), lambda b, i: (b * nb + i, 0)),
        out_shape=jax.ShapeDtypeStruct((bsz * seq, GROUP_W), F32),
        scratch_shapes=[pltpu.VMEM((SUBLANES, S5_LANES), F32)],
        compiler_params=_params("parallel", "arbitrary"),
        name="s5",
    )(p32, wb, wc, pw, d, w1, w2)


def _s5_tables(a_re, a_im, log_dt, b_re, b_im, c_re, c_im):
    dt = jnp.exp(log_dt)[:, None]
    mag = jnp.exp(a_re * dt)
    ab_re = mag * jnp.cos(a_im * dt)
    ab_im = mag * jnp.sin(a_im * dt)
    den = a_re * a_re + a_im * a_im
    num_re = ab_re - 1.0
    zoh_re = (num_re * a_re + ab_im * a_im) / den
    zoh_im = (ab_im * a_re - num_re * a_im) / den
    bb_re = zoh_re[..., None] * b_re - zoh_im[..., None] * b_im
    bb_im = zoh_re[..., None] * b_im + zoh_im[..., None] * b_re
    eye = jnp.eye(S5_GROUPS, dtype=F32)
    blk_in = lambda m: jnp.einsum("gpc,gh->gchp", m, eye).reshape(GROUP_W, S5_LANES)
    blk_out = lambda m: jnp.einsum("gcp,gh->gphc", m, eye).reshape(S5_LANES, GROUP_W)
    wb = jnp.concatenate([blk_in(bb_re), blk_in(bb_im)], axis=1).astype(BF16)
    wc = jnp.concatenate([blk_out(c_re), -blk_out(c_im)], axis=0).astype(BF16)

    def power(n):
        m = jnp.exp(n * (a_re * dt)[None])
        ang = n * (a_im * dt)[None]
        return ((m * jnp.cos(ang)).reshape(-1, S5_LANES), (m * jnp.sin(ang)).reshape(-1, S5_LANES))

    rows = jnp.arange(SUBLANES, dtype=F32)[:, None, None]
    tabs = []
    for shift in (1, 2, 4):
        pr, pi = power(jnp.full_like(rows, float(shift)))
        keep = (jnp.arange(SUBLANES) >= shift)[:, None]
        tabs += [jnp.where(keep, pr, 0.0), jnp.where(keep, pi, 0.0)]
    pr, pi = power(rows + 1.0)
    tabs += [pr, pi]
    return wb, wc, jnp.stack(tabs)


def _rope(t, c, s_up, s_dn):
    return (t * c + pltpu.roll(t, HEAD_DIM // 2, axis=1) * s_up
            + pltpu.roll(t, GROUP_W - HEAD_DIM // 2, axis=1) * s_dn)


def _ret_kernel(q_ref, k_ref, v_ref, g_ref, cos_ref, sup_ref, sdn_ref, intra_ref, qd_ref, kd_ref,
                cd_ref, hm_ref, bd_ref, seg_ref, o_ref, st_ref, *, tb):
    @pl.when(pl.program_id(1) == 0)
    def _():
        st_ref[...] = jnp.zeros_like(st_ref)

    for c in range(tb // RET_CHUNK):
        rows = slice(c * RET_CHUNK, (c + 1) * RET_CHUNK)
        cos = cos_ref[rows, :]
        sup = sup_ref[rows, :]
        sdn = sdn_ref[rows, :]
        q = _rope(q_ref[rows, :], cos, sup, sdn)
        k = _rope(k_ref[rows, :], cos, sup, sdn) * HEAD_DIM ** -0.5
        v = v_ref[rows, :]
        kb = k.astype(BF16)
        vb = v.astype(BF16)
        state = st_ref[...]
        o = _mm1(q * qd_ref[...], state)
        for h in range(N_HEADS):
            hm = hm_ref[h:h + 1, :]
            scores = _dot((q * hm).astype(BF16), kb, NT) * intra_ref[h]
            o = o + hm * _dot(scores.astype(BF16), vb)
        st_ref[...] = state * cd_ref[...] + bd_ref[...] * _dot((k * kd_ref[...]).astype(BF16), vb, TN)
        ms = _mm2(o * o, seg_ref[...]) * (1.0 / HEAD_DIM)
        o = o * lax.rsqrt(ms + NORM_EPS)
        g = g_ref[rows, :]
        o_ref[rows, :] = g * _sigmoid(g) * o


def _retention(p32, bsz, seq, tabs, consts, tb=512):
    nb = seq // tb
    col = lambda j: pl.BlockSpec((tb, GROUP_W), lambda b, i: (b * nb + i, j))
    pos = pl.BlockSpec((tb, GROUP_W), lambda b, i: (i, 0))
    cos, sup, sdn, intra, qd, kd, cd = tabs
    return pl.pallas_call(
        functools.partial(_ret_kernel, tb=tb),
        grid=(bsz, nb),
        in_specs=[col(P32_RET), col(P32_RET + 1), col(P32_RET + 2), col(P32_RET + 3), pos, pos, pos,
                  _const_spec((N_HEADS, RET_CHUNK, RET_CHUNK)),
                  _const_spec((RET_CHUNK, GROUP_W)), _const_spec((RET_CHUNK, GROUP_W)),
                  _const_spec((GROUP_W, GROUP_W)), _const_spec((N_HEADS, GROUP_W)),
                  _const_spec((GROUP_W, GROUP_W)), _const_spec((GROUP_W, GROUP_W))],
        out_specs=pl.BlockSpec((tb, GROUP_W), lambda b, i: (b * nb + i, 0)),
        out_shape=jax.ShapeDtypeStruct((bsz * seq, GROUP_W), F32),
        scratch_shapes=[pltpu.VMEM((GROUP_W, GROUP_W), F32)],
        compiler_params=_params("parallel", "arbitrary"),
        name="retention",
    )(p32, p32, p32, p32, cos, sup, sdn, intra, qd, kd, cd, consts["hm"], consts["bd"], consts["seg"])


def _retention_tables(seq):
    inv_freq = ROPE_BASE ** (-jnp.arange(0, HEAD_DIM, 2, dtype=F32) / HEAD_DIM)
    ang = jnp.arange(seq, dtype=F32)[:, None] * inv_freq[None, :]
    cos, sin = jnp.cos(ang), jnp.sin(ang)
    zero = jnp.zeros_like(sin)
    cos_t = jnp.tile(jnp.concatenate([cos, cos], axis=1), (1, N_HEADS))
    sup_t = jnp.tile(jnp.concatenate([zero, sin], axis=1), (1, N_HEADS))
    sdn_t = jnp.tile(jnp.concatenate([-sin, zero], axis=1), (1, N_HEADS))
    log_gamma = jnp.log1p(-jnp.exp2(-5.0 - jnp.arange(N_HEADS, dtype=F32)))
    idx = jnp.arange(RET_CHUNK, dtype=F32)
    rel = idx[:, None] - idx[None, :]
    intra = jnp.where(rel >= 0, jnp.exp(log_gamma[:, None, None] * jnp.maximum(rel, 0.0)), 0.0)
    lanes = jnp.repeat(log_gamma, HEAD_DIM)[None, :]
    qd = jnp.exp(lanes * (idx + 1.0)[:, None])
    kd = jnp.exp(lanes * (RET_CHUNK - 1.0 - idx)[:, None])
    cd = jnp.broadcast_to(jnp.exp(lanes * RET_CHUNK), (GROUP_W, GROUP_W))
    return cos_t, sup_t, sdn_t, intra, qd, kd, cd


def _sb_kernel(q_ref, k_ref, vt_ref, hm4_ref, cumt_ref, neg1_ref, trib_ref, trin_ref, o_ref, acc_ref):
    qb = pl.program_id(1)
    nsub = SB_Q // SB_BLOCK
    q = q_ref[...] * HEAD_DIM ** -0.5
    q4 = _tile4(q) * hm4_ref[...]
    cumt = cumt_ref[...]
    neg1 = neg1_ref[...]
    acc_ref[...] = jnp.zeros_like(acc_ref)
    heads = range(N_HEADS)
    dims = lambda h: slice(h * HEAD_DIM, (h + 1) * HEAD_DIM)

    def first_cols(x, f):
        head = f(x[:, :SB_BLOCK])
        return head if x.shape[1] == SB_BLOCK else jnp.concatenate([head, x[:, SB_BLOCK:]], axis=1)

    def add_blocks(tiles, runs):
        runs = list(runs)
        logits = []
        for blk, lo, hi, _ in tiles:
            s0 = pl.multiple_of(blk * SB_BLOCK, SB_BLOCK)
            logits += [_dot(k_ref[pl.ds(s0, SB_BLOCK), :], q4[h * SB_Q + lo:h * SB_Q + hi, :], NT) for h in heads]
        pending = []
        for t, (blk, lo, hi, diagonal) in enumerate(tiles):
            for h in heads:
                zt = logits[t * N_HEADS + h]
                sp = jnp.maximum(zt, 0.0) + jnp.log(1.0 + jnp.exp(-jnp.abs(zt)))
                log_w = zt - sp
                spb = sp.astype(BF16)
                if diagonal:
                    spb = first_cols(spb, lambda x: x * trib_ref[...])
                    log_w = first_cols(log_w, lambda x: x + trin_ref[...])
                run = runs[h][:, lo:hi]
                log_w = log_w + _dot(cumt, spb) + run
                run = run + _dot(neg1, spb)[0:1, :]
                pieces = ([runs[h][:, :lo]] if lo else []) + [run] + ([runs[h][:, hi:]] if hi < SB_Q else [])
                runs[h] = jnp.concatenate(pieces, axis=1)
                pending.append((blk, h, lo, hi, log_w))
        for blk, h, lo, hi, log_w in pending:
            acc_ref[dims(h), lo:hi] += _dot(vt_ref[blk, dims(h), :], jnp.exp(log_w).astype(BF16))
        return tuple(runs)

    runs = tuple(jnp.zeros((1, SB_Q), F32) for _ in heads)
    top = qb * nsub
    for m in range(nsub - 1, 0, -2):
        runs = add_blocks([(top + m, m * SB_BLOCK, SB_Q, True), (top + m - 1, (m - 1) * SB_BLOCK, SB_Q, True)], runs)

    def live(state):
        return jnp.logical_and(state[0] < qb * (nsub // 2), state[1] > SB_DEAD)

    def slowest(runs, lo):
        return jnp.max(jnp.maximum(jnp.maximum(runs[0][:, lo:], runs[1][:, lo:]),
                                   jnp.maximum(runs[2][:, lo:], runs[3][:, lo:])))

    def pair(state):
        jj, runs = state[0], state[2:]
        blk = top - 1 - 2 * jj
        both = lambda hi: lambda rs: add_blocks([(blk, 0, hi, False), (blk - 1, 0, hi, False)], rs)
        runs = lax.cond(slowest(runs, SB_Q // 2) > SB_DEAD, both(SB_Q), both(SB_Q // 2), runs)
        return (jj + 1, slowest(runs, 0)) + runs

    lax.while_loop(live, pair, (jnp.int32(0), jnp.float32(0.0)) + runs)
    o_ref[...] = acc_ref[...].T


def _stick_breaking(p16, vt, bsz, seq, consts):
    nq = seq // SB_Q
    nk = seq // SB_BLOCK
    return pl.pallas_call(
        _sb_kernel,
        grid=(bsz, nq),
        in_specs=[pl.BlockSpec((SB_Q, GROUP_W), lambda b, i: (b * nq + i, 0)),
                  pl.BlockSpec((seq, GROUP_W), lambda b, i: (b, 1)),
                  pl.BlockSpec((nk, GROUP_W, SB_BLOCK), lambda b, i: (b, 0, 0)),
                  _const_spec((N_HEADS * SB_Q, GROUP_W)),
                  _const_spec((SB_BLOCK, SB_BLOCK)),
                  _const_spec((SUBLANES, SB_BLOCK)),
                  _const_spec((SB_BLOCK, SB_BLOCK)),
                  _const_spec((SB_BLOCK, SB_BLOCK))],
        out_specs=pl.BlockSpec((SB_Q, GROUP_W), lambda b, i: (b * nq + i, 0)),
        out_shape=jax.ShapeDtypeStruct((bsz * seq, GROUP_W), F32),
        scratch_shapes=[pltpu.VMEM((GROUP_W, SB_Q), F32)],
        compiler_params=_params("parallel", "arbitrary"),
        name="stick_breaking",
    )(p16, p16, vt, consts["sb_hm4"], consts["sb_cumt"], consts["sb_neg1"], consts["sb_trib"], consts["sb_trin"])


def _tile4(x):
    return jnp.concatenate([x, x, x, x], axis=0)


def _fold4(x):
    n = x.shape[0] // N_HEADS
    return x[0:n] + x[n:2 * n] + x[2 * n:3 * n] + x[3 * n:4 * n]


def _shifted(cur, prev_ref, first_row):
    sh = jnp.where(first_row, prev_ref[0:1, :], pltpu.roll(cur, 1, axis=0))
    prev_ref[0:1, :] = cur[cur.shape[0] - 1:, :]
    return sh


def _rwkv_kernel(*refs, tb, has_vres):
    if has_vres:
        (cols_ref, vres_ref, vfirst_ref, mu_ref, vmu_ref, w0_ref, w2_ref, a0_ref, a2_ref, g2_ref,
         v0_ref, v2_ref, kk_ref, ka_ref, rk_ref, lnw_ref, lnb_ref,
         hm4_ref, sl_ref, il_ref, eye_ref, ltri_ref, seg_ref,
         o_ref, z_ref, prev_ref, prevv_ref) = refs
    else:
        (cols_ref, mu_ref, w0_ref, w2_ref, a0_ref, a2_ref, g2_ref,
         kk_ref, ka_ref, rk_ref, lnw_ref, lnb_ref,
         hm4_ref, sl_ref, il_ref, eye_ref, ltri_ref, seg_ref,
         o_ref, vout_ref, z_ref, prev_ref) = refs

    @pl.when(pl.program_id(1) == 0)
    def _():
        z_ref[...] = jnp.zeros_like(z_ref)
        prev_ref[...] = jnp.zeros_like(prev_ref)
        if has_vres:
            prevv_ref[...] = jnp.zeros_like(prevv_ref)

    first_row = lax.broadcasted_iota(jnp.int32, (tb, 1), 0) == 0
    cols = cols_ref[...]
    xs = cols + (_shifted(cols, prev_ref, first_row) - cols) * mu_ref[...]
    r = xs[:, 0:GROUP_W]
    k = xs[:, GROUP_W:2 * GROUP_W]
    v = xs[:, 2 * GROUP_W:3 * GROUP_W]
    wa = xs[:, 3 * GROUP_W:3 * GROUP_W + 128]
    gd = xs[:, 3 * GROUP_W + 128:4 * GROUP_W]
    w_log = -_softplus(-(w0_ref[...] + _mm1(jnp.tanh(wa), w2_ref[...]))) - 0.5
    lw = -jnp.exp(w_log)
    a = _sigmoid(a0_ref[...] + _mm1(wa, a2_ref[...]))
    g = _mm1(_sigmoid(gd), g2_ref[...])
    if has_vres:
        vr = vres_ref[...]
        vx = vr + (_shifted(vr, prevv_ref, first_row) - vr) * vmu_ref[...]
        v = v + (vfirst_ref[...] - v) * _sigmoid(v0_ref[...] + _mm1(vx, v2_ref[...]))
    else:
        vout_ref[...] = v
    seg = seg_ref[...]
    kk = k * kk_ref[...]
    kk = kk * lax.rsqrt(jnp.maximum(_mm2(kk * kk, seg), 1e-12))
    k = k * (1.0 + (a - 1.0) * ka_ref[...])
    aa = -kk
    bb = kk * a

    hm4 = hm4_ref[...]
    sl = sl_ref[...]
    il = il_ref[...]
    eye = eye_ref[...]

    w256 = GROUP_W
    ltri = ltri_ref[...]
    each = lambda f, *lists: [f(*xs) for xs in zip(*lists)]

    def chunk_maps(group):
        take = lambda x: [x[ci * RW_CHUNK:(ci + 1) * RW_CHUNK, :] for ci in group]
        rc, kc, vc, ac, bc, lwc = take(r), take(k), take(v), take(aa), take(bb), take(lw)
        c = each(lambda x: _mm2l(ltri, x), lwc)
        c_end = each(lambda x: x[RW_CHUNK - 1:, :], c)
        at4 = each(lambda a_, c_, l_: _tile4(a_ * jnp.exp(c_ - l_)) * hm4, ac, c, lwc)
        rt4 = each(lambda r_, c_: _tile4(r_ * jnp.exp(c_)) * hm4, rc, c)
        at4b = each(lambda x: x.astype(BF16), at4)
        be4b = each(lambda b_, c_, e_: (_tile4(b_ * jnp.exp(e_ - c_)) * hm4).astype(BF16), bc, c, c_end)
        ke4b = each(lambda k_, c_, e_: (_tile4(k_ * jnp.exp(e_ - c_)) * hm4).astype(BF16), kc, c, c_end)
        v4b = each(lambda v_: (_tile4(v_) * hm4).astype(BF16), vc)
        lhs = each(lambda a_, r_: jnp.concatenate([a_, r_.astype(BF16)], axis=0), at4b, rt4)
        rhs = each(lambda b_, k_, c_: jnp.concatenate([_tile4((b_ * jnp.exp(-c_)).astype(BF16)),
                                                        _tile4((k_ * jnp.exp(-c_)).astype(BF16))], axis=0),
                   bc, kc, c)
        nn = each(lambda l_, r_: _dot(l_, r_, NT), lhs, rhs)
        n_ab = each(lambda x: x[:w256, :w256] * sl, nn)
        n_ak = each(lambda x: (x[:w256, w256:] * sl).astype(BF16), nn)
        n_rb = each(lambda x: (x[w256:, :w256] * il).astype(BF16), nn)
        n_rk = each(lambda x: (x[w256:, w256:] * il).astype(BF16), nn)
        yield None
        inv = each(lambda x: eye + x, n_ab)
        pw = each(lambda x: x.astype(BF16), n_ab)
        for _ in range(int(math.log2(RW_CHUNK)) - 1):
            pw = each(lambda x: _dot(x, x).astype(BF16), pw)
            inv = each(lambda i_, p_: i_ + _dot(p_, i_.astype(BF16)), inv, pw)
            yield None
        akv = each(lambda n_, v_: _dot(n_, v_).astype(BF16), n_ak, v4b)
        ugb = each(lambda i_, x_, a_: _dot(i_.astype(BF16), jnp.concatenate([x_, a_], axis=1)).astype(BF16),
                   inv, akv, at4b)
        yield None
        yh = each(_dot, n_rb, ugb)
        y0 = each(lambda y_, n_, v_: y_[:, :w256] + _dot(n_, v_), yh, n_rk, v4b)
        hmat = each(lambda r_, y_: (r_ + y_[:, w256:]).astype(BF16), rt4, yh)
        yield None
        qp = each(lambda b_, u_: _dot(b_, u_, TN), be4b, ugb)
        qm = each(lambda q_, k_, v_: q_[:, :w256] + _dot(k_, v_, TN), qp, ke4b, v4b)
        p = each(lambda e_, q_: (eye * jnp.exp(e_) + q_[:, w256:]).astype(BF16), c_end, qp)
        yield list(zip(y0, hmat, p, qm))

    z = z_ref[...]
    ys = []

    def state_step(chunk_map):
        nonlocal z
        y0, hmat, p, qm = chunk_map
        zb = z.astype(BF16)
        ys.append(_fold4(y0 + _dot(hmat, zb)))
        z = _dot(p, zb) + qm

    n_chunks = tb // RW_CHUNK
    waiting = []
    for first in range(0, n_chunks, RW_GROUP):
        for stage, maps in enumerate(chunk_maps(range(first, min(first + RW_GROUP, n_chunks)))):
            if waiting and stage % 2 == 1:
                state_step(waiting.pop(0))
        for chunk_map in waiting:
            state_step(chunk_map)
        waiting = maps
    for chunk_map in waiting:
        state_step(chunk_map)
    z_ref[...] = z

    y = jnp.concatenate(ys, axis=0)
    inv_n = 1.0 / HEAD_DIM
    mean = _mm2(y, seg) * inv_n
    yc = y - mean
    var = _mm2(yc * yc, seg) * inv_n
    y = yc * lax.rsqrt(var + RWKV_GN_EPS) * lnw_ref[...] + lnb_ref[...]
    bonus = _mm2(r * k * rk_ref[...], seg) * v
    o_ref[...] = (y + bonus) * g


def _rwkv(p32, vfirst, bsz, seq, lw, consts, has_vres, tb=512):
    nb = seq // tb
    rows = lambda w, j: pl.BlockSpec((tb, w), lambda b, i: (b * nb + i, j))
    vec = _const_spec((1, GROUP_W))
    mat = lambda n: _const_spec((n, GROUP_W))
    big = _const_spec((GROUP_W, GROUP_W))
    cnames = ("hm4", "sl", "il", "eye", "ltri", "seg")
    cspecs = [big, big, big, big, _const_spec((RW_CHUNK, RW_CHUNK)), big]
    cargs = [consts[n] for n in cnames]
    cols_spec = pl.BlockSpec((tb, 4 * GROUP_W), lambda b, i: (b * nb + i, 0))
    out_spec = pl.BlockSpec((tb, GROUP_W), lambda b, i: (b * nb + i, 0))
    out_sds = jax.ShapeDtypeStruct((bsz * seq, GROUP_W), F32)
    scratch = [pltpu.VMEM((GROUP_W, GROUP_W), F32), pltpu.VMEM((SUBLANES, 4 * GROUP_W), F32)]
    if has_vres:
        in_specs = ([cols_spec, rows(GROUP_W, P32_VRES), rows(GROUP_W, 0), _const_spec((1, 4 * GROUP_W)), vec,
                     vec, mat(128), vec, mat(128), mat(RWKV_LORA_G), vec, mat(GROUP_W), vec, vec, vec, vec, vec]
                    + cspecs)
        args = [p32, p32, vfirst, lw["mu"], lw["vmu"], lw["w0"], lw["w2"], lw["a0"], lw["a2"],
                lw["g2"], lw["v0"], lw["v2"], lw["k_k"], lw["k_a"], lw["r_k"], lw["ln_w"], lw["ln_b"]] + cargs
        out_specs, out_shape = out_spec, out_sds
        scratch = scratch + [pltpu.VMEM((SUBLANES, GROUP_W), F32)]
    else:
        in_specs = ([cols_spec, _const_spec((1, 4 * GROUP_W)), vec, mat(128), vec, mat(128), mat(RWKV_LORA_G),
                     vec, vec, vec, vec, vec] + cspecs)
        args = [p32, lw["mu"], lw["w0"], lw["w2"], lw["a0"], lw["a2"], lw["g2"],
                lw["k_k"], lw["k_a"], lw["r_k"], lw["ln_w"], lw["ln_b"]] + cargs
        out_specs, out_shape = [out_spec, out_spec], [out_sds, out_sds]
    return pl.pallas_call(
        functools.partial(_rwkv_kernel, tb=tb, has_vres=has_vres),
        grid=(bsz, nb),
        in_specs=in_specs,
        out_specs=out_specs,
        out_shape=out_shape,
        scratch_shapes=scratch,
        compiler_params=_params("parallel", "arbitrary"),
        name="rwkv7",
    )(*args)


def _constants():
    lane_head = jnp.arange(GROUP_W) // HEAD_DIM
    hm = (lane_head[None, :] == jnp.arange(N_HEADS)[:, None]).astype(F32)
    bd = (lane_head[:, None] == lane_head[None, :])
    seg = bd.astype(BF16)
    pos = jnp.arange(GROUP_W) % RW_CHUNK
    hm4 = bd.astype(F32)
    sl = (bd & (pos[None, :] < pos[:, None])).astype(F32)
    il = (bd & (pos[None, :] <= pos[:, None])).astype(F32)
    eye = jnp.eye(GROUP_W, dtype=F32)
    t = jnp.arange(RW_CHUNK)
    ltri = (t[None, :] <= t[:, None]).astype(BF16)
    j = jnp.arange(SB_BLOCK)
    sb_cumt = -(j[None, :] > j[:, None]).astype(BF16)
    sb_neg1 = -jnp.ones((SUBLANES, SB_BLOCK), BF16)
    sb_hm4 = jnp.repeat(hm, SB_Q, axis=0).astype(BF16)
    visible = j[:, None] < j[None, :]
    sb_trib = visible.astype(BF16)
    sb_trin = jnp.where(visible, 0.0, SB_MASKED).astype(F32)
    return dict(hm=hm, bd=bd.astype(F32), seg=seg, hm4=hm4, sl=sl, il=il, eye=eye, ltri=ltri,
                sb_cumt=sb_cumt, sb_neg1=sb_neg1, sb_hm4=sb_hm4, sb_trib=sb_trib, sb_trin=sb_trin)


def kernel(x, norm_mix_pre, norm_mix_post, norm_ffn_pre, norm_ffn_post, w_in_first, w_in_rest, w_out,
           s5_a_re, s5_a_im, s5_log_dt, s5_b_re, s5_b_im, s5_c_re, s5_c_im, s5_d, s5_glu_w1, s5_glu_w2,
           rw_mu, rw_vres_mu, rw_w0, rw_w2, rw_a0, rw_a2, rw_g2, rw_v0, rw_v2,
           rw_k_k, rw_k_a, rw_r_k, rw_ln_w, rw_ln_b, w_up, w_down):
    bsz, seq, _ = x.shape
    consts = _constants()
    ret_tabs = _retention_tables(seq)
    x2 = x.reshape(bsz * seq, D_MODEL)
    row = lambda vec: vec.reshape(1, -1)
    zpad = lambda m, n: jnp.pad(m, ((0, n - m.shape[0]), (0, 0)))
    v_first = None
    for l in range(DEPTH):
        w_in = w_in_first if l == 0 else w_in_rest[l - 1]
        parts = [w_in[:, OFF_RW:N_IN0], w_in[:, OFF_RET:OFF_SB], w_in[:, :OFF_RET]]
        if l > 0:
            parts.append(jnp.pad(w_in[:, N_IN0:], ((0, 0), (0, GROUP_W - RWKV_LORA_V))))
        n32 = sum(p.shape[1] for p in parts)
        w_cat = jnp.concatenate(parts + [w_in[:, OFF_SB:OFF_RW]], axis=1).astype(BF16)
        p32, p16, vt = _in_proj(x2, row(norm_mix_pre[l]), w_cat, n32, 2 * GROUP_W)

        wb, wc, pw = _s5_tables(s5_a_re[l], s5_a_im[l], s5_log_dt[l], s5_b_re[l], s5_b_im[l],
                                s5_c_re[l], s5_c_im[l])
        out_s5 = _s5(p32, bsz, seq, wb, wc, pw, row(s5_d[l]),
                     s5_glu_w1[l].astype(BF16), s5_glu_w2[l].astype(BF16))
        out_ret = _retention(p32, bsz, seq, ret_tabs, consts)
        out_sb = _stick_breaking(p16, vt, bsz, seq, consts)

        lw = dict(
            mu=row(rw_mu[l]), w0=row(rw_w0[l]), a0=row(rw_a0[l]),
            w2=zpad(rw_w2[l], 128).astype(BF16),
            a2=jnp.pad(rw_a2[l], ((RWKV_LORA_W, 0), (0, 0))).astype(BF16),
            g2=rw_g2[l].astype(BF16),
            k_k=row(rw_k_k[l]), k_a=row(rw_k_a[l]), r_k=row(rw_r_k[l]),
            ln_w=row(rw_ln_w[l]), ln_b=row(rw_ln_b[l]))
        if l == 0:
            out_rw, v_first = _rwkv(p32, None, bsz, seq, lw, consts, False)
        else:
            lw.update(vmu=row(jnp.pad(rw_vres_mu[l - 1], (0, GROUP_W - RWKV_LORA_V))),
                      v0=row(rw_v0[l - 1]), v2=zpad(rw_v2[l - 1], GROUP_W).astype(BF16))
            out_rw = _rwkv(p32, v_first, bsz, seq, lw, consts, True)

        x2 = _post((out_s5, out_ret, out_rw, out_sb), x2, w_out[l].astype(BF16), row(norm_mix_post[l]),
                   row(norm_ffn_pre[l]), w_up[l].astype(BF16), w_down[l].astype(BF16), row(norm_ffn_post[l]))
    return x2.reshape(bsz, seq, D_MODEL)
```

```python
import functools
import math

import jax
import jax.numpy as jnp
from jax import lax
from jax.experimental import pallas as pl
from jax.experimental.pallas import tpu as pltpu

F32 = jnp.float32
BF16 = jnp.bfloat16

D_MODEL = 1024
DEPTH = 4
GROUP_W = 256
HEAD_DIM = 64
N_HEADS = 4
S5_GROUP_CH = 16
S5_GROUPS = 16
S5_STATE = 64
S5_LANES = S5_GROUPS * S5_STATE
S5_SUB = 128
RET_CHUNK = 128
ROPE_BASE = 10000.0
RWKV_LORA_W = 64
RWKV_LORA_A = 64
RWKV_LORA_V = 32
RWKV_LORA_G = 128
RWKV_GN_EPS = 64e-5
RW_CHUNK = 64
RW_GROUP = 4
SB_BLOCK = 128
SB_Q = 512
SB_DEAD = -110.0
SB_MASKED = -1e30
D_FF = 4 * D_MODEL
NORM_EPS = 1e-6

OFF_RET = GROUP_W
OFF_SB = OFF_RET + 4 * GROUP_W
OFF_RW = OFF_SB + 3 * GROUP_W
N_IN0 = OFF_RW + 4 * GROUP_W

P32_RET = 4
P32_S5 = 8
P32_VRES = 9

SUBLANES = 8
VMEM_LIMIT = 56 * 1024 * 1024

NN = (((1,), (0,)), ((), ()))
NT = (((1,), (1,)), ((), ()))
TN = (((0,), (0,)), ((), ()))


def _dot(a, b, dims=NN):
    return lax.dot_general(a, b, dims, preferred_element_type=F32)


def _split(x):
    hi = x.astype(BF16)
    lo = (x - hi.astype(F32)).astype(BF16)
    return hi, lo


def _mm2(a, b_exact, dims=NN):
    ah, al = _split(a)
    return _dot(ah, b_exact, dims) + _dot(al, b_exact, dims)


def _mm2l(a_exact, b, dims=NN):
    bh, bl = _split(b)
    return _dot(a_exact, bh, dims) + _dot(a_exact, bl, dims)


def _mm1(a, b, dims=NN):
    return _dot(a.astype(BF16), b.astype(BF16), dims)


def _softplus(x):
    return jnp.maximum(x, 0.0) + jnp.log(1.0 + jnp.exp(-jnp.abs(x)))


def _sigmoid(x):
    return 1.0 / (1.0 + jnp.exp(-x))


def _rms(x, gain):
    return x * lax.rsqrt(jnp.mean(x * x, axis=-1, keepdims=True) + NORM_EPS) * gain


def _params(*sem):
    return pltpu.CompilerParams(dimension_semantics=sem, vmem_limit_bytes=VMEM_LIMIT)


def _const_spec(shape):
    zeros = (0,) * len(shape)
    return pl.BlockSpec(shape, lambda *_: zeros)


def _in_proj_kernel(x_ref, g_ref, w_ref, o32_ref, o16_ref, vt_ref, *, n32, n16, tn):
    h = _rms(x_ref[...], g_ref[...]).astype(BF16)
    for c0 in range(0, n32, tn):
        o32_ref[:, c0:c0 + tn] = _dot(h, w_ref[:, c0:c0 + tn])
    for c0 in range(0, n16, tn):
        o16_ref[:, c0:c0 + tn] = _dot(h, w_ref[:, n32 + c0:n32 + c0 + tn]).astype(BF16)
    v = _dot(h, w_ref[:, n32 + n16:])
    for j in range(vt_ref.shape[0]):
        vt_ref[j] = v[j * SB_BLOCK:(j + 1) * SB_BLOCK, :].T.astype(BF16)


def _in_proj(x2, gain, w, n32, n16, tm=512, tn=256):
    t = x2.shape[0]
    kern = functools.partial(_in_proj_kernel, n32=n32, n16=n16, tn=tn)
    return pl.pallas_call(
        kern,
        grid=(t // tm,),
        in_specs=[pl.BlockSpec((tm, D_MODEL), lambda i: (i, 0)),
                  _const_spec((1, D_MODEL)),
                  _const_spec((D_MODEL, n32 + n16 + GROUP_W))],
        out_specs=[pl.BlockSpec((tm, n32), lambda i: (i, 0)),
                   pl.BlockSpec((tm, n16), lambda i: (i, 0)),
                   pl.BlockSpec((tm // SB_BLOCK, GROUP_W, SB_BLOCK), lambda i: (i, 0, 0))],
        out_shape=[jax.ShapeDtypeStruct((t, n32), F32),
                   jax.ShapeDtypeStruct((t, n16), BF16),
                   jax.ShapeDtypeStruct((t // SB_BLOCK, GROUP_W, SB_BLOCK), BF16)],
        compiler_params=_params("parallel"),
        name="in_proj",
    )(x2, gain, w)


def _post_kernel(m0_ref, m1_ref, m2_ref, m3_ref, x_ref, wo_ref, gmix_ref, gpre_ref,
                 wup_ref, wdn_ref, gpost_ref, o_ref, acc_ref, *, tf):
    mixed = (_mm1(m0_ref[...], wo_ref[0:GROUP_W, :])
             + _mm1(m1_ref[...], wo_ref[GROUP_W:2 * GROUP_W, :])
             + _mm1(m2_ref[...], wo_ref[2 * GROUP_W:3 * GROUP_W, :])
             + _mm1(m3_ref[...], wo_ref[3 * GROUP_W:4 * GROUP_W, :]))
    x1 = x_ref[...] + _rms(mixed, gmix_ref[...])
    h = _rms(x1, gpre_ref[...]).astype(BF16)
    for c0 in range(0, D_FF, tf):
        a = jnp.maximum(_dot(h, wup_ref[:, c0:c0 + tf]), 0.0)
        part = _dot((a * a).astype(BF16), wdn_ref[c0:c0 + tf, :])
        if c0 == 0:
            acc_ref[...] = part
        else:
            acc_ref[...] += part
    o_ref[...] = x1 + _rms(acc_ref[...], gpost_ref[...])


def _post(mix_outs, x2, wo, gmix, gpre, wup, wdn, gpost, tm=1024, tf=512):
    t = x2.shape[0]
    row = lambda w: pl.BlockSpec((tm, w), lambda i: (i, 0))
    single = lambda shape: pl.BlockSpec(shape, lambda i: (0, 0), pipeline_mode=pl.Buffered(1))
    return pl.pallas_call(
        functools.partial(_post_kernel, tf=tf),
        grid=(t // tm,),
        in_specs=[row(GROUP_W)] * 4 + [row(D_MODEL), single((D_MODEL, D_MODEL)),
                  _const_spec((1, D_MODEL)), _const_spec((1, D_MODEL)),
                  single((D_MODEL, D_FF)), single((D_FF, D_MODEL)), _const_spec((1, D_MODEL))],
        out_specs=row(D_MODEL),
        out_shape=jax.ShapeDtypeStruct((t, D_MODEL), F32),
        scratch_shapes=[pltpu.VMEM((tm, D_MODEL), F32)],
        compiler_params=_params("parallel"),
        name="out_proj_ffn",
    )(*mix_outs, x2, wo, gmix, gpre, wup, wdn, gpost)


def _s5_kernel(u_ref, wb_ref, wc_ref, pw_ref, d_ref, w1_ref, w2_ref, o_ref, st_ref, *, tb):
    @pl.when(pl.program_id(1) == 0)
    def _():
        st_ref[...] = jnp.zeros_like(st_ref)

    pieces = [slice(c * S5_SUB, (c + 1) * S5_SUB) for c in range(tb // S5_SUB)]
    us = [u_ref[rows, :] for rows in pieces]
    bus = [_mm1(u, wb_ref[...]) for u in us]
    cr, ci = st_ref[0:1, :], st_ref[1:2, :]
    for rows, u, bu in zip(pieces, us, bus):
        xrs, xis = [], []
        for j in range(S5_SUB // SUBLANES):
            xr = bu[j * SUBLANES:(j + 1) * SUBLANES, :S5_LANES]
            xi = bu[j * SUBLANES:(j + 1) * SUBLANES, S5_LANES:]
            for n, shift in enumerate((1, 2, 4)):
                pr = pw_ref[2 * n]
                pi = pw_ref[2 * n + 1]
                sr = pltpu.roll(xr, shift, axis=0)
                si = pltpu.roll(xi, shift, axis=0)
                xr, xi = xr + pr * sr - pi * si, xi + pr * si + pi * sr
            pr = pw_ref[6]
            pi = pw_ref[7]
            xr, xi = xr + pr * cr - pi * ci, xi + pr * ci + pi * cr
            cr, ci = xr[SUBLANES - 1:SUBLANES, :], xi[SUBLANES - 1:SUBLANES, :]
            xrs.append(xr)
            xis.append(xi)
        y = (_mm1(jnp.concatenate(xrs, axis=0), wc_ref[0:S5_LANES, :])
             + _mm1(jnp.concatenate(xis, axis=0), wc_ref[S5_LANES:, :]) + d_ref[...] * u)
        y = jax.nn.gelu(y).astype(BF16)
        o_ref[rows, :] = _dot(y, w1_ref[...]) * _sigmoid(_dot(y, w2_ref[...]))
    st_ref[0:1, :] = cr
    st_ref[1:2, :] = ci


def _s5(p32, bsz, seq, wb, wc, pw, d, w1, w2, tb=512):
    nb = seq // tb
    return pl.pallas_call(
        functools.partial(_s5_kernel, tb=tb),
        grid=(bsz, nb),
        in_specs=[pl.BlockSpec((tb, GROUP_W), lambda b, i: (b * nb + i, P32_S5)),
                  _const_spec((GROUP_W, 2 * S5_LANES)),
                  _const_spec((2 * S5_LANES, GROUP_W)),
                  _const_spec((8, SUBLANES, S5_LANES)),
                  _const_spec((1, GROUP_W)),
                  _const_spec((GROUP_W, GROUP_W)),
                  _const_spec((GROUP_W, GROUP_W))],
        out_specs=pl.BlockSpec((tb, GROUP_W), lambda b, i: (b * nb + i, 0)),
        out_shape=jax.ShapeDtypeStruct((bsz * seq, GROUP_W), F32),
        scratch_shapes=[pltpu.VMEM((SUBLANES, S5_LANES), F32)],
        compiler_params=_params("parallel", "arbitrary"),
        name="s5",
    )(p32, wb, wc, pw, d, w1, w2)


def _s5_tables(a_re, a_im, log_dt, b_re, b_im, c_re, c_im):
    dt = jnp.exp(log_dt)[:, None]
    mag = jnp.exp(a_re * dt)
    ab_re = mag * jnp.cos(a_im * dt)
    ab_im = mag * jnp.sin(a_im * dt)
    den = a_re * a_re + a_im * a_im
    num_re = ab_re - 1.0
    zoh_re = (num_re * a_re + ab_im * a_im) / den
    zoh_im = (ab_im * a_re - num_re * a_im) / den
    bb_re = zoh_re[..., None] * b_re - zoh_im[..., None] * b_im
    bb_im = zoh_re[..., None] * b_im + zoh_im[..., None] * b_re
    eye = jnp.eye(S5_GROUPS, dtype=F32)
    blk_in = lambda m: jnp.einsum("gpc,gh->gchp", m, eye).reshape(GROUP_W, S5_LANES)
    blk_out = lambda m: jnp.einsum("gcp,gh->gphc", m, eye).reshape(S5_LANES, GROUP_W)
    wb = jnp.concatenate([blk_in(bb_re), blk_in(bb_im)], axis=1).astype(BF16)
    wc = jnp.concatenate([blk_out(c_re), -blk_out(c_im)], axis=0).astype(BF16)

    def power(n):
        m = jnp.exp(n * (a_re * dt)[None])
        ang = n * (a_im * dt)[None]
        return ((m * jnp.cos(ang)).reshape(-1, S5_LANES), (m * jnp.sin(ang)).reshape(-1, S5_LANES))

    rows = jnp.arange(SUBLANES, dtype=F32)[:, None, None]
    tabs = []
    for shift in (1, 2, 4):
        pr, pi = power(jnp.full_like(rows, float(shift)))
        keep = (jnp.arange(SUBLANES) >= shift)[:, None]
        tabs += [jnp.where(keep, pr, 0.0), jnp.where(keep, pi, 0.0)]
    pr, pi = power(rows + 1.0)
    tabs += [pr, pi]
    return wb, wc, jnp.stack(tabs)


def _rope(t, c, s_up, s_dn):
    return (t * c + pltpu.roll(t, HEAD_DIM // 2, axis=1) * s_up
            + pltpu.roll(t, GROUP_W - HEAD_DIM // 2, axis=1) * s_dn)


def _ret_kernel(q_ref, k_ref, v_ref, g_ref, cos_ref, sup_ref, sdn_ref, intra_ref, qd_ref, kd_ref,
                cd_ref, hm_ref, bd_ref, seg_ref, o_ref, st_ref, *, tb):
    @pl.when(pl.program_id(1) == 0)
    def _():
        st_ref[...] = jnp.zeros_like(st_ref)

    for c in range(tb // RET_CHUNK):
        rows = slice(c * RET_CHUNK, (c + 1) * RET_CHUNK)
        cos = cos_ref[rows, :]
        sup = sup_ref[rows, :]
        sdn = sdn_ref[rows, :]
        q = _rope(q_ref[rows, :], cos, sup, sdn)
        k = _rope(k_ref[rows, :], cos, sup, sdn) * HEAD_DIM ** -0.5
        v = v_ref[rows, :]
        kb = k.astype(BF16)
        vb = v.astype(BF16)
        state = st_ref[...]
        o = _mm1(q * qd_ref[...], state)
        for h in range(N_HEADS):
            hm = hm_ref[h:h + 1, :]
            scores = _dot((q * hm).astype(BF16), kb, NT) * intra_ref[h]
            o = o + hm * _dot(scores.astype(BF16), vb)
        st_ref[...] = state * cd_ref[...] + bd_ref[...] * _dot((k * kd_ref[...]).astype(BF16), vb, TN)
        ms = _mm2(o * o, seg_ref[...]) * (1.0 / HEAD_DIM)
        o = o * lax.rsqrt(ms + NORM_EPS)
        g = g_ref[rows, :]
        o_ref[rows, :] = g * _sigmoid(g) * o


def _retention(p32, bsz, seq, tabs, consts, tb=512):
    nb = seq // tb
    col = lambda j: pl.BlockSpec((tb, GROUP_W), lambda b, i: (b * nb + i, j))
    pos = pl.BlockSpec((tb, GROUP_W), lambda b, i: (i, 0))
    cos, sup, sdn, intra, qd, kd, cd = tabs
    return pl.pallas_call(
        functools.partial(_ret_kernel, tb=tb),
        grid=(bsz, nb),
        in_specs=[col(P32_RET), col(P32_RET + 1), col(P32_RET + 2), col(P32_RET + 3), pos, pos, pos,
                  _const_spec((N_HEADS, RET_CHUNK, RET_CHUNK)),
                  _const_spec((RET_CHUNK, GROUP_W)), _const_spec((RET_CHUNK, GROUP_W)),
                  _const_spec((GROUP_W, GROUP_W)), _const_spec((N_HEADS, GROUP_W)),
                  _const_spec((GROUP_W, GROUP_W)), _const_spec((GROUP_W, GROUP_W))],
        out_specs=pl.BlockSpec((tb, GROUP_W), lambda b, i: (b * nb + i, 0)),
        out_shape=jax.ShapeDtypeStruct((bsz * seq, GROUP_W), F32),
        scratch_shapes=[pltpu.VMEM((GROUP_W, GROUP_W), F32)],
        compiler_params=_params("parallel", "arbitrary"),
        name="retention",
    )(p32, p32, p32, p32, cos, sup, sdn, intra, qd, kd, cd, consts["hm"], consts["bd"], consts["seg"])


def _retention_tables(seq):
    inv_freq = ROPE_BASE ** (-jnp.arange(0, HEAD_DIM, 2, dtype=F32) / HEAD_DIM)
    ang = jnp.arange(seq, dtype=F32)[:, None] * inv_freq[None, :]
    cos, sin = jnp.cos(ang), jnp.sin(ang)
    zero = jnp.zeros_like(sin)
    cos_t = jnp.tile(jnp.concatenate([cos, cos], axis=1), (1, N_HEADS))
    sup_t = jnp.tile(jnp.concatenate([zero, sin], axis=1), (1, N_HEADS))
    sdn_t = jnp.tile(jnp.concatenate([-sin, zero], axis=1), (1, N_HEADS))
    log_gamma = jnp.log1p(-jnp.exp2(-5.0 - jnp.arange(N_HEADS, dtype=F32)))
    idx = jnp.arange(RET_CHUNK, dtype=F32)
    rel = idx[:, None] - idx[None, :]
    intra = jnp.where(rel >= 0, jnp.exp(log_gamma[:, None, None] * jnp.maximum(rel, 0.0)), 0.0)
    lanes = jnp.repeat(log_gamma, HEAD_DIM)[None, :]
    qd = jnp.exp(lanes * (idx + 1.0)[:, None])
    kd = jnp.exp(lanes * (RET_CHUNK - 1.0 - idx)[:, None])
    cd = jnp.broadcast_to(jnp.exp(lanes * RET_CHUNK), (GROUP_W, GROUP_W))
    return cos_t, sup_t, sdn_t, intra, qd, kd, cd


def _sb_kernel(q_ref, k_ref, vt_ref, hm4_ref, cumt_ref, neg1_ref, trib_ref, trin_ref, o_ref, acc_ref):
    qb = pl.program_id(1)
    nsub = SB_Q // SB_BLOCK
    q = q_ref[...] * HEAD_DIM ** -0.5
    q4 = _tile4(q) * hm4_ref[...]
    cumt = cumt_ref[...]
    neg1 = neg1_ref[...]
    acc_ref[...] = jnp.zeros_like(acc_ref)
    heads = range(N_HEADS)
    dims = lambda h: slice(h * HEAD_DIM, (h + 1) * HEAD_DIM)

    def first_cols(x, f):
        head = f(x[:, :SB_BLOCK])
        return head if x.shape[1] == SB_BLOCK else jnp.concatenate([head, x[:, SB_BLOCK:]], axis=1)

    def add_blocks(tiles, runs):
        runs = list(runs)
        logits = []
        for blk, lo, hi, _ in tiles:
            s0 = pl.multiple_of(blk * SB_BLOCK, SB_BLOCK)
            logits += [_dot(k_ref[pl.ds(s0, SB_BLOCK), :], q4[h * SB_Q + lo:h * SB_Q + hi, :], NT) for h in heads]
        pending = []
        for t, (blk, lo, hi, diagonal) in enumerate(tiles):
            for h in heads:
                zt = logits[t * N_HEADS + h]
                sp = jnp.maximum(zt, 0.0) + jnp.log(1.0 + jnp.exp(-jnp.abs(zt)))
                log_w = zt - sp
                spb = sp.astype(BF16)
                if diagonal:
                    spb = first_cols(spb, lambda x: x * trib_ref[...])
                    log_w = first_cols(log_w, lambda x: x + trin_ref[...])
                run = runs[h][:, lo:hi]
                log_w = log_w + _dot(cumt, spb) + run
                run = run + _dot(neg1, spb)[0:1, :]
                pieces = ([runs[h][:, :lo]] if lo else []) + [run] + ([runs[h][:, hi:]] if hi < SB_Q else [])
                runs[h] = jnp.concatenate(pieces, axis=1)
                pending.append((blk, h, lo, hi, log_w))
        for blk, h, lo, hi, log_w in pending:
            acc_ref[dims(h), lo:hi] += _dot(vt_ref[blk, dims(h), :], jnp.exp(log_w).astype(BF16))
        return tuple(runs)

    runs = tuple(jnp.zeros((1, SB_Q), F32) for _ in heads)
    top = qb * nsub
    for m in range(nsub - 1, 0, -2):
        runs = add_blocks([(top + m, m * SB_BLOCK, SB_Q, True), (top + m - 1, (m - 1) * SB_BLOCK, SB_Q, True)], runs)

    def live(state):
        return jnp.logical_and(state[0] < qb * (nsub // 2), state[1] > SB_DEAD)

    def slowest(runs, lo):
        return jnp.max(jnp.maximum(jnp.maximum(runs[0][:, lo:], runs[1][:, lo:]),
                                   jnp.maximum(runs[2][:, lo:], runs[3][:, lo:])))

    def pair(state):
        jj, runs = state[0], state[2:]
        blk = top - 1 - 2 * jj
        both = lambda hi: lambda rs: add_blocks([(blk, 0, hi, False), (blk - 1, 0, hi, False)], rs)
        runs = lax.cond(slowest(runs, SB_Q // 2) > SB_DEAD, both(SB_Q), both(SB_Q // 2), runs)
        return (jj + 1, slowest(runs, 0)) + runs

    lax.while_loop(live, pair, (jnp.int32(0), jnp.float32(0.0)) + runs)
    o_ref[...] = acc_ref[...].T


def _stick_breaking(p16, vt, bsz, seq, consts):
    nq = seq // SB_Q
    nk = seq // SB_BLOCK
    return pl.pallas_call(
        _sb_kernel,
        grid=(bsz, nq),
        in_specs=[pl.BlockSpec((SB_Q, GROUP_W), lambda b, i: (b * nq + i, 0)),
                  pl.BlockSpec((seq, GROUP_W), lambda b, i: (b, 1)),
                  pl.BlockSpec((nk, GROUP_W, SB_BLOCK), lambda b, i: (b, 0, 0)),
                  _const_spec((N_HEADS * SB_Q, GROUP_W)),
                  _const_spec((SB_BLOCK, SB_BLOCK)),
                  _const_spec((SUBLANES, SB_BLOCK)),
                  _const_spec((SB_BLOCK, SB_BLOCK)),
                  _const_spec((SB_BLOCK, SB_BLOCK))],
        out_specs=pl.BlockSpec((SB_Q, GROUP_W), lambda b, i: (b * nq + i, 0)),
        out_shape=jax.ShapeDtypeStruct((bsz * seq, GROUP_W), F32),
        scratch_shapes=[pltpu.VMEM((GROUP_W, SB_Q), F32)],
        compiler_params=_params("parallel", "arbitrary"),
        name="stick_breaking",
    )(p16, p16, vt, consts["sb_hm4"], consts["sb_cumt"], consts["sb_neg1"], consts["sb_trib"], consts["sb_trin"])


def _tile4(x):
    return jnp.concatenate([x, x, x, x], axis=0)


def _fold4(x):
    n = x.shape[0] // N_HEADS
    return x[0:n] + x[n:2 * n] + x[2 * n:3 * n] + x[3 * n:4 * n]


def _shifted(cur, prev_ref, first_row):
    sh = jnp.where(first_row, prev_ref[0:1, :], pltpu.roll(cur, 1, axis=0))
    prev_ref[0:1, :] = cur[cur.shape[0] - 1:, :]
    return sh


def _rwkv_kernel(*refs, tb, has_vres):
    if has_vres:
        (cols_ref, vres_ref, vfirst_ref, mu_ref, vmu_ref, w0_ref, w2_ref, a0_ref, a2_ref, g2_ref,
         v0_ref, v2_ref, kk_ref, ka_ref, rk_ref, lnw_ref, lnb_ref,
         hm4_ref, sl_ref, il_ref, eye_ref, ltri_ref, seg_ref,
         o_ref, z_ref, prev_ref, prevv_ref) = refs
    else:
        (cols_ref, mu_ref, w0_ref, w2_ref, a0_ref, a2_ref, g2_ref,
         kk_ref, ka_ref, rk_ref, lnw_ref, lnb_ref,
         hm4_ref, sl_ref, il_ref, eye_ref, ltri_ref, seg_ref,
         o_ref, vout_ref, z_ref, prev_ref) = refs

    @pl.when(pl.program_id(1) == 0)
    def _():
        z_ref[...] = jnp.zeros_like(z_ref)
        prev_ref[...] = jnp.zeros_like(prev_ref)
        if has_vres:
            prevv_ref[...] = jnp.zeros_like(prevv_ref)

    first_row = lax.broadcasted_iota(jnp.int32, (tb, 1), 0) == 0
    cols = cols_ref[...]
    xs = cols + (_shifted(cols, prev_ref, first_row) - cols) * mu_ref[...]
    r = xs[:, 0:GROUP_W]
    k = xs[:, GROUP_W:2 * GROUP_W]
    v = xs[:, 2 * GROUP_W:3 * GROUP_W]
    wa = xs[:, 3 * GROUP_W:3 * GROUP_W + 128]
    gd = xs[:, 3 * GROUP_W + 128:4 * GROUP_W]
    w_log = -_softplus(-(w0_ref[...] + _mm1(jnp.tanh(wa), w2_ref[...]))) - 0.5
    lw = -jnp.exp(w_log)
    a = _sigmoid(a0_ref[...] + _mm1(wa, a2_ref[...]))
    g = _mm1(_sigmoid(gd), g2_ref[...])
    if has_vres:
        vr = vres_ref[...]
        vx = vr + (_shifted(vr, prevv_ref, first_row) - vr) * vmu_ref[...]
        v = v + (vfirst_ref[...] - v) * _sigmoid(v0_ref[...] + _mm1(vx, v2_ref[...]))
    else:
        vout_ref[...] = v
    seg = seg_ref[...]
    kk = k * kk_ref[...]
    kk = kk * lax.rsqrt(jnp.maximum(_mm2(kk * kk, seg), 1e-12))
    k = k * (1.0 + (a - 1.0) * ka_ref[...])
    aa = -kk
    bb = kk * a

    hm4 = hm4_ref[...]
    sl = sl_ref[...]
    il = il_ref[...]
    eye = eye_ref[...]

    w256 = GROUP_W
    ltri = ltri_ref[...]
    each = lambda f, *lists: [f(*xs) for xs in zip(*lists)]

    def chunk_maps(group):
        take = lambda x: [x[ci * RW_CHUNK:(ci + 1) * RW_CHUNK, :] for ci in group]
        rc, kc, vc, ac, bc, lwc = take(r), take(k), take(v), take(aa), take(bb), take(lw)
        c = each(lambda x: _mm2l(ltri, x), lwc)
        c_end = each(lambda x: x[RW_CHUNK - 1:, :], c)
        at4 = each(lambda a_, c_, l_: _tile4(a_ * jnp.exp(c_ - l_)) * hm4, ac, c, lwc)
        rt4 = each(lambda r_, c_: _tile4(r_ * jnp.exp(c_)) * hm4, rc, c)
        at4b = each(lambda x: x.astype(BF16), at4)
        be4b = each(lambda b_, c_, e_: (_tile4(b_ * jnp.exp(e_ - c_)) * hm4).astype(BF16), bc, c, c_end)
        ke4b = each(lambda k_, c_, e_: (_tile4(k_ * jnp.exp(e_ - c_)) * hm4).astype(BF16), kc, c, c_end)
        v4b = each(lambda v_: (_tile4(v_) * hm4).astype(BF16), vc)
        lhs = each(lambda a_, r_: jnp.concatenate([a_, r_.astype(BF16)], axis=0), at4b, rt4)
        rhs = each(lambda b_, k_, c_: jnp.concatenate([_tile4((b_ * jnp.exp(-c_)).astype(BF16)),
                                                        _tile4((k_ * jnp.exp(-c_)).astype(BF16))], axis=0),
                   bc, kc, c)
        nn = each(lambda l_, r_: _dot(l_, r_, NT), lhs, rhs)
        n_ab = each(lambda x: x[:w256, :w256] * sl, nn)
        n_ak = each(lambda x: (x[:w256, w256:] * sl).astype(BF16), nn)
        n_rb = each(lambda x: (x[w256:, :w256] * il).astype(BF16), nn)
        n_rk = each(lambda x: (x[w256:, w256:] * il).astype(BF16), nn)
        yield None
        inv = each(lambda x: eye + x, n_ab)
        pw = each(lambda x: x.astype(BF16), n_ab)
        for _ in range(int(math.log2(RW_CHUNK)) - 1):
            pw = each(lambda x: _dot(x, x).astype(BF16), pw)
            inv = each(lambda i_, p_: i_ + _dot(p_, i_.astype(BF16)), inv, pw)
            yield None
        akv = each(lambda n_, v_: _dot(n_, v_).astype(BF16), n_ak, v4b)
        ugb = each(lambda i_, x_, a_: _dot(i_.astype(BF16), jnp.concatenate([x_, a_], axis=1)).astype(BF16),
                   inv, akv, at4b)
        yield None
        yh = each(_dot, n_rb, ugb)
        y0 = each(lambda y_, n_, v_: y_[:, :w256] + _dot(n_, v_), yh, n_rk, v4b)
        hmat = each(lambda r_, y_: (r_ + y_[:, w256:]).astype(BF16), rt4, yh)
        yield None
        qp = each(lambda b_, u_: _dot(b_, u_, TN), be4b, ugb)
        qm = each(lambda q_, k_, v_: q_[:, :w256] + _dot(k_, v_, TN), qp, ke4b, v4b)
        p = each(lambda e_, q_: (eye * jnp.exp(e_) + q_[:, w256:]).astype(BF16), c_end, qp)
        yield list(zip(y0, hmat, p, qm))

    z = z_ref[...]
    ys = []

    def state_step(chunk_map):
        nonlocal z
        y0, hmat, p, qm = chunk_map
        zb = z.astype(BF16)
        ys.append(_fold4(y0 + _dot(hmat, zb)))
        z = _dot(p, zb) + qm

    n_chunks = tb // RW_CHUNK
    waiting = []
    for first in range(0, n_chunks, RW_GROUP):
        for stage, maps in enumerate(chunk_maps(range(first, min(first + RW_GROUP, n_chunks)))):
            if waiting and stage % 2 == 1:
                state_step(waiting.pop(0))
        for chunk_map in waiting:
            state_step(chunk_map)
        waiting = maps
    for chunk_map in waiting:
        state_step(chunk_map)
    z_ref[...] = z

    y = jnp.concatenate(ys, axis=0)
    inv_n = 1.0 / HEAD_DIM
    mean = _mm2(y, seg) * inv_n
    yc = y - mean
    var = _mm2(yc * yc, seg) * inv_n
    y = yc * lax.rsqrt(var + RWKV_GN_EPS) * lnw_ref[...] + lnb_ref[...]
    bonus = _mm2(r * k * rk_ref[...], seg) * v
    o_ref[...] = (y + bonus) * g


def _rwkv(p32, vfirst, bsz, seq, lw, consts, has_vres, tb=1024):
    nb = seq // tb
    rows = lambda w, j: pl.BlockSpec((tb, w), lambda b, i: (b * nb + i, j))
    vec = _const_spec((1, GROUP_W))
    mat = lambda n: _const_spec((n, GROUP_W))
    big = _const_spec((GROUP_W, GROUP_W))
    cnames = ("hm4", "sl", "il", "eye", "ltri", "seg")
    cspecs = [big, big, big, big, _const_spec((RW_CHUNK, RW_CHUNK)), big]
    cargs = [consts[n] for n in cnames]
    cols_spec = pl.BlockSpec((tb, 4 * GROUP_W), lambda b, i: (b * nb + i, 0))
    out_spec = pl.BlockSpec((tb, GROUP_W), lambda b, i: (b * nb + i, 0))
    out_sds = jax.ShapeDtypeStruct((bsz * seq, GROUP_W), F32)
    scratch = [pltpu.VMEM((GROUP_W, GROUP_W), F32), pltpu.VMEM((SUBLANES, 4 * GROUP_W), F32)]
    if has_vres:
        in_specs = ([cols_spec, rows(GROUP_W, P32_VRES), rows(GROUP_W, 0), _const_spec((1, 4 * GROUP_W)), vec,
                     vec, mat(128), vec, mat(128), mat(RWKV_LORA_G), vec, mat(GROUP_W), vec, vec, vec, vec, vec]
                    + cspecs)
        args = [p32, p32, vfirst, lw["mu"], lw["vmu"], lw["w0"], lw["w2"], lw["a0"], lw["a2"],
                lw["g2"], lw["v0"], lw["v2"], lw["k_k"], lw["k_a"], lw["r_k"], lw["ln_w"], lw["ln_b"]] + cargs
        out_specs, out_shape = out_spec, out_sds
        scratch = scratch + [pltpu.VMEM((SUBLANES, GROUP_W), F32)]
    else:
        in_specs = ([cols_spec, _const_spec((1, 4 * GROUP_W)), vec, mat(128), vec, mat(128), mat(RWKV_LORA_G),
                     vec, vec, vec, vec, vec] + cspecs)
        args = [p32, lw["mu"], lw["w0"], lw["w2"], lw["a0"], lw["a2"], lw["g2"],
                lw["k_k"], lw["k_a"], lw["r_k"], lw["ln_w"], lw["ln_b"]] + cargs
        out_specs, out_shape = [out_spec, out_spec], [out_sds, out_sds]
    return pl.pallas_call(
        functools.partial(_rwkv_kernel, tb=tb, has_vres=has_vres),
        grid=(bsz, nb),
        in_specs=in_specs,
        out_specs=out_specs,
        out_shape=out_shape,
        scratch_shapes=scratch,
        compiler_params=_params("parallel", "arbitrary"),
        name="rwkv7",
    )(*args)


def _constants():
    lane_head = jnp.arange(GROUP_W) // HEAD_DIM
    hm = (lane_head[None, :] == jnp.arange(N_HEADS)[:, None]).astype(F32)
    bd = (lane_head[:, None] == lane_head[None, :])
    seg = bd.astype(BF16)
    pos = jnp.arange(GROUP_W) % RW_CHUNK
    hm4 = bd.astype(F32)
    sl = (bd & (pos[None, :] < pos[:, None])).astype(F32)
    il = (bd & (pos[None, :] <= pos[:, None])).astype(F32)
    eye = jnp.eye(GROUP_W, dtype=F32)
    t = jnp.arange(RW_CHUNK)
    ltri = (t[None, :] <= t[:, None]).astype(BF16)
    j = jnp.arange(SB_BLOCK)
    sb_cumt = -(j[None, :] > j[:, None]).astype(BF16)
    sb_neg1 = -jnp.ones((SUBLANES, SB_BLOCK), BF16)
    sb_hm4 = jnp.repeat(hm, SB_Q, axis=0).astype(BF16)
    visible = j[:, None] < j[None, :]
    sb_trib = visible.astype(BF16)
    sb_trin = jnp.where(visible, 0.0, SB_MASKED).astype(F32)
    return dict(hm=hm, bd=bd.astype(F32), seg=seg, hm4=hm4, sl=sl, il=il, eye=eye, ltri=ltri,
                sb_cumt=sb_cumt, sb_neg1=sb_neg1, sb_hm4=sb_hm4, sb_trib=sb_trib, sb_trin=sb_trin)


def kernel(x, norm_mix_pre, norm_mix_post, norm_ffn_pre, norm_ffn_post, w_in_first, w_in_rest, w_out,
           s5_a_re, s5_a_im, s5_log_dt, s5_b_re, s5_b_im, s5_c_re, s5_c_im, s5_d, s5_glu_w1, s5_glu_w2,
           rw_mu, rw_vres_mu, rw_w0, rw_w2, rw_a0, rw_a2, rw_g2, rw_v0, rw_v2,
           rw_k_k, rw_k_a, rw_r_k, rw_ln_w, rw_ln_b, w_up, w_down):
    bsz, seq, _ = x.shape
    consts = _constants()
    ret_tabs = _retention_tables(seq)
    x2 = x.reshape(bsz * seq, D_MODEL)
    row = lambda vec: vec.reshape(1, -1)
    zpad = lambda m, n: jnp.pad(m, ((0, n - m.shape[0]), (0, 0)))
    v_first = None
    for l in range(DEPTH):
        w_in = w_in_first if l == 0 else w_in_rest[l - 1]
        parts = [w_in[:, OFF_RW:N_IN0], w_in[:, OFF_RET:OFF_SB], w_in[:, :OFF_RET]]
        if l > 0:
            parts.append(jnp.pad(w_in[:, N_IN0:], ((0, 0), (0, GROUP_W - RWKV_LORA_V))))
        n32 = sum(p.shape[1] for p in parts)
        w_cat = jnp.concatenate(parts + [w_in[:, OFF_SB:OFF_RW]], axis=1).astype(BF16)
        p32, p16, vt = _in_proj(x2, row(norm_mix_pre[l]), w_cat, n32, 2 * GROUP_W)

        wb, wc, pw = _s5_tables(s5_a_re[l], s5_a_im[l], s5_log_dt[l], s5_b_re[l], s5_b_im[l],
                                s5_c_re[l], s5_c_im[l])
        out_s5 = _s5(p32, bsz, seq, wb, wc, pw, row(s5_d[l]),
                     s5_glu_w1[l].astype(BF16), s5_glu_w2[l].astype(BF16))
        out_ret = _retention(p32, bsz, seq, ret_tabs, consts)
        out_sb = _stick_breaking(p16, vt, bsz, seq, consts)

        lw = dict(
            mu=row(rw_mu[l]), w0=row(rw_w0[l]), a0=row(rw_a0[l]),
            w2=zpad(rw_w2[l], 128).astype(BF16),
            a2=jnp.pad(rw_a2[l], ((RWKV_LORA_W, 0), (0, 0))).astype(BF16),
            g2=rw_g2[l].astype(BF16),
            k_k=row(rw_k_k[l]), k_a=row(rw_k_a[l]), r_k=row(rw_r_k[l]),
            ln_w=row(rw_ln_w[l]), ln_b=row(rw_ln_b[l]))
        if l == 0:
            out_rw, v_first = _rwkv(p32, None, bsz, seq, lw, consts, False)
        else:
            lw.update(vmu=row(jnp.pad(rw_vres_mu[l - 1], (0, GROUP_W - RWKV_LORA_V))),
                      v0=row(rw_v0[l - 1]), v2=zpad(rw_v2[l - 1], GROUP_W).astype(BF16))
            out_rw = _rwkv(p32, v_first, bsz, seq, lw, consts, True)

        x2 = _post((out_s5, out_ret, out_rw, out_sb), x2, w_out[l].astype(BF16), row(norm_mix_post[l]),
                   row(norm_ffn_pre[l]), w_up[l].astype(BF16), w_down[l].astype(BF16), row(norm_ffn_post[l]))
    return x2.reshape(bsz, seq, D_MODEL)
```

```python
import functools
import math

import jax
import jax.numpy as jnp
from jax import lax
from jax.experimental import pallas as pl
from jax.experimental.pallas import tpu as pltpu

F32 = jnp.float32
BF16 = jnp.bfloat16

D_MODEL = 1024
DEPTH = 4
GROUP_W = 256
HEAD_DIM = 64
N_HEADS = 4
S5_GROUP_CH = 16
S5_GROUPS = 16
S5_STATE = 64
S5_LANES = S5_GROUPS * S5_STATE
S5_SUB = 128
RET_CHUNK = 128
ROPE_BASE = 10000.0
RWKV_LORA_W = 64
RWKV_LORA_A = 64
RWKV_LORA_V = 32
RWKV_LORA_G = 128
RWKV_GN_EPS = 64e-5
RW_CHUNK = 64
RW_GROUP = 4
SB_BLOCK = 128
SB_Q = 512
SB_DEAD = -110.0
SB_MASKED = -1e30
D_FF = 4 * D_MODEL
NORM_EPS = 1e-6

OFF_RET = GROUP_W
OFF_SB = OFF_RET + 4 * GROUP_W
OFF_RW = OFF_SB + 3 * GROUP_W
N_IN0 = OFF_RW + 4 * GROUP_W

P32_RET = 4
P32_S5 = 8
P32_VRES = 9

SUBLANES = 8
VMEM_LIMIT = 56 * 1024 * 1024

NN = (((1,), (0,)), ((), ()))
NT = (((1,), (1,)), ((), ()))
TN = (((0,), (0,)), ((), ()))


def _dot(a, b, dims=NN):
    return lax.dot_general(a, b, dims, preferred_element_type=F32)


def _split(x):
    hi = x.astype(BF16)
    lo = (x - hi.astype(F32)).astype(BF16)
    return hi, lo


def _mm2(a, b_exact, dims=NN):
    ah, al = _split(a)
    return _dot(ah, b_exact, dims) + _dot(al, b_exact, dims)


def _mm2l(a_exact, b, dims=NN):
    bh, bl = _split(b)
    return _dot(a_exact, bh, dims) + _dot(a_exact, bl, dims)


def _mm1(a, b, dims=NN):
    return _dot(a.astype(BF16), b.astype(BF16), dims)


def _softplus(x):
    return jnp.maximum(x, 0.0) + jnp.log(1.0 + jnp.exp(-jnp.abs(x)))


def _sigmoid(x):
    return 1.0 / (1.0 + jnp.exp(-x))


def _rms(x, gain):
    return x * lax.rsqrt(jnp.mean(x * x, axis=-1, keepdims=True) + NORM_EPS) * gain


def _params(*sem):
    return pltpu.CompilerParams(dimension_semantics=sem, vmem_limit_bytes=VMEM_LIMIT)


def _const_spec(shape):
    zeros = (0,) * len(shape)
    return pl.BlockSpec(shape, lambda *_: zeros)


def _in_proj_kernel(x_ref, g_ref, w_ref, o32_ref, o16_ref, vt_ref, *, n32, n16, tn):
    h = _rms(x_ref[...], g_ref[...]).astype(BF16)
    for c0 in range(0, n32, tn):
        o32_ref[:, c0:c0 + tn] = _dot(h, w_ref[:, c0:c0 + tn])
    for c0 in range(0, n16, tn):
        o16_ref[:, c0:c0 + tn] = _dot(h, w_ref[:, n32 + c0:n32 + c0 + tn]).astype(BF16)
    v = _dot(h, w_ref[:, n32 + n16:])
    for j in range(vt_ref.shape[0]):
        vt_ref[j] = v[j * SB_BLOCK:(j + 1) * SB_BLOCK, :].T.astype(BF16)


def _in_proj(x2, gain, w, n32, n16, tm=512, tn=256):
    t = x2.shape[0]
    kern = functools.partial(_in_proj_kernel, n32=n32, n16=n16, tn=tn)
    return pl.pallas_call(
        kern,
        grid=(t // tm,),
        in_specs=[pl.BlockSpec((tm, D_MODEL), lambda i: (i, 0)),
                  _const_spec((1, D_MODEL)),
                  _const_spec((D_MODEL, n32 + n16 + GROUP_W))],
        out_specs=[pl.BlockSpec((tm, n32), lambda i: (i, 0)),
                   pl.BlockSpec((tm, n16), lambda i: (i, 0)),
                   pl.BlockSpec((tm // SB_BLOCK, GROUP_W, SB_BLOCK), lambda i: (i, 0, 0))],
        out_shape=[jax.ShapeDtypeStruct((t, n32), F32),
                   jax.ShapeDtypeStruct((t, n16), BF16),
                   jax.ShapeDtypeStruct((t // SB_BLOCK, GROUP_W, SB_BLOCK), BF16)],
        compiler_params=_params("parallel"),
        name="in_proj",
    )(x2, gain, w)


def _post_kernel(m0_ref, m1_ref, m2_ref, m3_ref, x_ref, wo_ref, gmix_ref, gpre_ref,
                 wup_ref, wdn_ref, gpost_ref, o_ref, acc_ref, *, tf):
    mixed = (_mm1(m0_ref[...], wo_ref[0:GROUP_W, :])
             + _mm1(m1_ref[...], wo_ref[GROUP_W:2 * GROUP_W, :])
             + _mm1(m2_ref[...], wo_ref[2 * GROUP_W:3 * GROUP_W, :])
             + _mm1(m3_ref[...], wo_ref[3 * GROUP_W:4 * GROUP_W, :]))
    x1 = x_ref[...] + _rms(mixed, gmix_ref[...])
    h = _rms(x1, gpre_ref[...]).astype(BF16)
    for c0 in range(0, D_FF, tf):
        a = jnp.maximum(_dot(h, wup_ref[:, c0:c0 + tf]), 0.0)
        part = _dot((a * a).astype(BF16), wdn_ref[c0:c0 + tf, :])
        if c0 == 0:
            acc_ref[...] = part
        else:
            acc_ref[...] += part
    o_ref[...] = x1 + _rms(acc_ref[...], gpost_ref[...])


def _post(mix_outs, x2, wo, gmix, gpre, wup, wdn, gpost, tm=1024, tf=512):
    t = x2.shape[0]
    row = lambda w: pl.BlockSpec((tm, w), lambda i: (i, 0))
    single = lambda shape: pl.BlockSpec(shape, lambda i: (0, 0), pipeline_mode=pl.Buffered(1))
    return pl.pallas_call(
        functools.partial(_post_kernel, tf=tf),
        grid=(t // tm,),
        in_specs=[row(GROUP_W)] * 4 + [row(D_MODEL), single((D_MODEL, D_MODEL)),
                  _const_spec((1, D_MODEL)), _const_spec((1, D_MODEL)),
                  single((D_MODEL, D_FF)), single((D_FF, D_MODEL)), _const_spec((1, D_MODEL))],
        out_specs=row(D_MODEL),
        out_shape=jax.ShapeDtypeStruct((t, D_MODEL), F32),
        scratch_shapes=[pltpu.VMEM((tm, D_MODEL), F32)],
        compiler_params=_params("parallel"),
        name="out_proj_ffn",
    )(*mix_outs, x2, wo, gmix, gpre, wup, wdn, gpost)


def _s5_kernel(u_ref, wb_ref, wc_ref, pw_ref, d_ref, w1_ref, w2_ref, o_ref, st_ref, *, tb):
    @pl.when(pl.program_id(1) == 0)
    def _():
        st_ref[...] = jnp.zeros_like(st_ref)

    pieces = [slice(c * S5_SUB, (c + 1) * S5_SUB) for c in range(tb // S5_SUB)]
    us = [u_ref[rows, :] for rows in pieces]
    bus = [_mm1(u, wb_ref[...]) for u in us]
    cr, ci = st_ref[0:1, :], st_ref[1:2, :]
    for rows, u, bu in zip(pieces, us, bus):
        xrs, xis = [], []
        for j in range(S5_SUB // SUBLANES):
            xr = bu[j * SUBLANES:(j + 1) * SUBLANES, :S5_LANES]
            xi = bu[j * SUBLANES:(j + 1) * SUBLANES, S5_LANES:]
            for n, shift in enumerate((1, 2, 4)):
                pr = pw_ref[2 * n]
                pi = pw_ref[2 * n + 1]
                sr = pltpu.roll(xr, shift, axis=0)
                si = pltpu.roll(xi, shift, axis=0)
                xr, xi = xr + pr * sr - pi * si, xi + pr * si + pi * sr
            pr = pw_ref[6]
            pi = pw_ref[7]
            xr, xi = xr + pr * cr - pi * ci, xi + pr * ci + pi * cr
            cr, ci = xr[SUBLANES - 1:SUBLANES, :], xi[SUBLANES - 1:SUBLANES, :]
            xrs.append(xr)
            xis.append(xi)
        y = (_mm1(jnp.concatenate(xrs, axis=0), wc_ref[0:S5_LANES, :])
             + _mm1(jnp.concatenate(xis, axis=0), wc_ref[S5_LANES:, :]) + d_ref[...] * u)
        y = jax.nn.gelu(y).astype(BF16)
        o_ref[rows, :] = _dot(y, w1_ref[...]) * _sigmoid(_dot(y, w2_ref[...]))
    st_ref[0:1, :] = cr
    st_ref[1:2, :] = ci


def _s5(p32, bsz, seq, wb, wc, pw, d, w1, w2, tb=1024):
    nb = seq // tb
    return pl.pallas_call(
        functools.partial(_s5_kernel, tb=tb),
        grid=(bsz, nb),
        in_specs=[pl.BlockSpec((tb, GROUP_W), lambda b, i: (b * nb + i, P32_S5)),
                  _const_spec((GROUP_W, 2 * S5_LANES)),
                  _const_spec((2 * S5_LANES, GROUP_W)),
                  _const_spec((8, SUBLANES, S5_LANES)),
                  _const_spec((1, GROUP_W)),
                  _const_spec((GROUP_W, GROUP_W)),
                  _const_spec((GROUP_W, GROUP_W))],
        out_specs=pl.BlockSpec((tb, GROUP_W), lambda b, i: (b * nb + i, 0)),
        out_shape=jax.ShapeDtypeStruct((bsz * seq, GROUP_W), F32),
        scratch_shapes=[pltpu.VMEM((SUBLANES, S5_LANES), F32)],
        compiler_params=_params("parallel", "arbitrary"),
        name="s5",
    )(p32, wb, wc, pw, d, w1, w2)


def _s5_tables(a_re, a_im, log_dt, b_re, b_im, c_re, c_im):
    dt = jnp.exp(log_dt)[:, None]
    mag = jnp.exp(a_re * dt)
    ab_re = mag * jnp.cos(a_im * dt)
    ab_im = mag * jnp.sin(a_im * dt)
    den = a_re * a_re + a_im * a_im
    num_re = ab_re - 1.0
    zoh_re = (num_re * a_re + ab_im * a_im) / den
    zoh_im = (ab_im * a_re - num_re * a_im) / den
    bb_re = zoh_re[..., None] * b_re - zoh_im[..., None] * b_im
    bb_im = zoh_re[..., None] * b_im + zoh_im[..., None] * b_re
    eye = jnp.eye(S5_GROUPS, dtype=F32)
    blk_in = lambda m: jnp.einsum("gpc,gh->gchp", m, eye).reshape(GROUP_W, S5_LANES)
    blk_out = lambda m: jnp.einsum("gcp,gh->gphc", m, eye).reshape(S5_LANES, GROUP_W)
    wb = jnp.concatenate([blk_in(bb_re), blk_in(bb_im)], axis=1).astype(BF16)
    wc = jnp.concatenate([blk_out(c_re), -blk_out(c_im)], axis=0).astype(BF16)

    def power(n):
        m = jnp.exp(n * (a_re * dt)[None])
        ang = n * (a_im * dt)[None]
        return ((m * jnp.cos(ang)).reshape(-1, S5_LANES), (m * jnp.sin(ang)).reshape(-1, S5_LANES))

    rows = jnp.arange(SUBLANES, dtype=F32)[:, None, None]
    tabs = []
    for shift in (1, 2, 4):
        pr, pi = power(jnp.full_like(rows, float(shift)))
        keep = (jnp.arange(SUBLANES) >= shift)[:, None]
        tabs += [jnp.where(keep, pr, 0.0), jnp.where(keep, pi, 0.0)]
    pr, pi = power(rows + 1.0)
    tabs += [pr, pi]
    return wb, wc, jnp.stack(tabs)


def _rope(t, c, s_up, s_dn):
    return (t * c + pltpu.roll(t, HEAD_DIM // 2, axis=1) * s_up
            + pltpu.roll(t, GROUP_W - HEAD_DIM // 2, axis=1) * s_dn)


def _ret_kernel(q_ref, k_ref, v_ref, g_ref, cos_ref, sup_ref, sdn_ref, intra_ref, qd_ref, kd_ref,
                cd_ref, hm_ref, bd_ref, seg_ref, o_ref, st_ref, *, tb):
    @pl.when(pl.program_id(1) == 0)
    def _():
        st_ref[...] = jnp.zeros_like(st_ref)

    for c in range(tb // RET_CHUNK):
        rows = slice(c * RET_CHUNK, (c + 1) * RET_CHUNK)
        cos = cos_ref[rows, :]
        sup = sup_ref[rows, :]
        sdn = sdn_ref[rows, :]
        q = _rope(q_ref[rows, :], cos, sup, sdn)
        k = _rope(k_ref[rows, :], cos, sup, sdn) * HEAD_DIM ** -0.5
        v = v_ref[rows, :]
        kb = k.astype(BF16)
        vb = v.astype(BF16)
        state = st_ref[...]
        o = _mm1(q * qd_ref[...], state)
        for h in range(N_HEADS):
            hm = hm_ref[h:h + 1, :]
            scores = _dot((q * hm).astype(BF16), kb, NT) * intra_ref[h]
            o = o + hm * _dot(scores.astype(BF16), vb)
        st_ref[...] = state * cd_ref[...] + bd_ref[...] * _dot((k * kd_ref[...]).astype(BF16), vb, TN)
        ms = _mm2(o * o, seg_ref[...]) * (1.0 / HEAD_DIM)
        o = o * lax.rsqrt(ms + NORM_EPS)
        g = g_ref[rows, :]
        o_ref[rows, :] = g * _sigmoid(g) * o


def _retention(p32, bsz, seq, tabs, consts, tb=1024):
    nb = seq // tb
    col = lambda j: pl.BlockSpec((tb, GROUP_W), lambda b, i: (b * nb + i, j))
    pos = pl.BlockSpec((tb, GROUP_W), lambda b, i: (i, 0))
    cos, sup, sdn, intra, qd, kd, cd = tabs
    return pl.pallas_call(
        functools.partial(_ret_kernel, tb=tb),
        grid=(bsz, nb),
        in_specs=[col(P32_RET), col(P32_RET + 1), col(P32_RET + 2), col(P32_RET + 3), pos, pos, pos,
                  _const_spec((N_HEADS, RET_CHUNK, RET_CHUNK)),
                  _const_spec((RET_CHUNK, GROUP_W)), _const_spec((RET_CHUNK, GROUP_W)),
                  _const_spec((GROUP_W, GROUP_W)), _const_spec((N_HEADS, GROUP_W)),
                  _const_spec((GROUP_W, GROUP_W)), _const_spec((GROUP_W, GROUP_W))],
        out_specs=pl.BlockSpec((tb, GROUP_W), lambda b, i: (b * nb + i, 0)),
        out_shape=jax.ShapeDtypeStruct((bsz * seq, GROUP_W), F32),
        scratch_shapes=[pltpu.VMEM((GROUP_W, GROUP_W), F32)],
        compiler_params=_params("parallel", "arbitrary"),
        name="retention",
    )(p32, p32, p32, p32, cos, sup, sdn, intra, qd, kd, cd, consts["hm"], consts["bd"], consts["seg"])


def _retention_tables(seq):
    inv_freq = ROPE_BASE ** (-jnp.arange(0, HEAD_DIM, 2, dtype=F32) / HEAD_DIM)
    ang = jnp.arange(seq, dtype=F32)[:, None] * inv_freq[None, :]
    cos, sin = jnp.cos(ang), jnp.sin(ang)
    zero = jnp.zeros_like(sin)
    cos_t = jnp.tile(jnp.concatenate([cos, cos], axis=1), (1, N_HEADS))
    sup_t = jnp.tile(jnp.concatenate([zero, sin], axis=1), (1, N_HEADS))
    sdn_t = jnp.tile(jnp.concatenate([-sin, zero], axis=1), (1, N_HEADS))
    log_gamma = jnp.log1p(-jnp.exp2(-5.0 - jnp.arange(N_HEADS, dtype=F32)))
    idx = jnp.arange(RET_CHUNK, dtype=F32)
    rel = idx[:, None] - idx[None, :]
    intra = jnp.where(rel >= 0, jnp.exp(log_gamma[:, None, None] * jnp.maximum(rel, 0.0)), 0.0)
    lanes = jnp.repeat(log_gamma, HEAD_DIM)[None, :]
    qd = jnp.exp(lanes * (idx + 1.0)[:, None])
    kd = jnp.exp(lanes * (RET_CHUNK - 1.0 - idx)[:, None])
    cd = jnp.broadcast_to(jnp.exp(lanes * RET_CHUNK), (GROUP_W, GROUP_W))
    return cos_t, sup_t, sdn_t, intra, qd, kd, cd


def _sb_kernel(q_ref, k_ref, vt_ref, hm4_ref, cumt_ref, neg1_ref, trib_ref, trin_ref, o_ref, acc_ref):
    qb = pl.program_id(1)
    nsub = SB_Q // SB_BLOCK
    q = q_ref[...] * HEAD_DIM ** -0.5
    q4 = _tile4(q) * hm4_ref[...]
    cumt = cumt_ref[...]
    neg1 = neg1_ref[...]
    acc_ref[...] = jnp.zeros_like(acc_ref)
    heads = range(N_HEADS)
    dims = lambda h: slice(h * HEAD_DIM, (h + 1) * HEAD_DIM)

    def first_cols(x, f):
        head = f(x[:, :SB_BLOCK])
        return head if x.shape[1] == SB_BLOCK else jnp.concatenate([head, x[:, SB_BLOCK:]], axis=1)

    def add_blocks(tiles, runs):
        runs = list(runs)
        logits = []
        for blk, lo, hi, _ in tiles:
            s0 = pl.multiple_of(blk * SB_BLOCK, SB_BLOCK)
            logits += [_dot(k_ref[pl.ds(s0, SB_BLOCK), :], q4[h * SB_Q + lo:h * SB_Q + hi, :], NT) for h in heads]
        pending = []
        for t, (blk, lo, hi, diagonal) in enumerate(tiles):
            for h in heads:
                zt = logits[t * N_HEADS + h]
                sp = jnp.maximum(zt, 0.0) + jnp.log(1.0 + jnp.exp(-jnp.abs(zt)))
                log_w = zt - sp
                spb = sp.astype(BF16)
                if diagonal:
                    spb = first_cols(spb, lambda x: x * trib_ref[...])
                    log_w = first_cols(log_w, lambda x: x + trin_ref[...])
                run = runs[h][:, lo:hi]
                log_w = log_w + _dot(cumt, spb) + run
                run = run + _dot(neg1, spb)[0:1, :]
                pieces = ([runs[h][:, :lo]] if lo else []) + [run] + ([runs[h][:, hi:]] if hi < SB_Q else [])
                runs[h] = jnp.concatenate(pieces, axis=1)
                pending.append((blk, h, lo, hi, log_w))
        for blk, h, lo, hi, log_w in pending:
            acc_ref[dims(h), lo:hi] += _dot(vt_ref[blk, dims(h), :], jnp.exp(log_w).astype(BF16))
        return tuple(runs)

    runs = tuple(jnp.zeros((1, SB_Q), F32) for _ in heads)
    top = qb * nsub
    for m in range(nsub - 1, 0, -2):
        runs = add_blocks([(top + m, m * SB_BLOCK, SB_Q, True), (top + m - 1, (m - 1) * SB_BLOCK, SB_Q, True)], runs)

    def live(state):
        return jnp.logical_and(state[0] < qb * (nsub // 2), state[1] > SB_DEAD)

    def slowest(runs, lo):
        return jnp.max(jnp.maximum(jnp.maximum(runs[0][:, lo:], runs[1][:, lo:]),
                                   jnp.maximum(runs[2][:, lo:], runs[3][:, lo:])))

    def pair(state):
        jj, runs = state[0], state[2:]
        blk = top - 1 - 2 * jj
        both = lambda hi: lambda rs: add_blocks([(blk, 0, hi, False), (blk - 1, 0, hi, False)], rs)
        runs = lax.cond(slowest(runs, SB_Q // 2) > SB_DEAD, both(SB_Q), both(SB_Q // 2), runs)
        return (jj + 1, slowest(runs, 0)) + runs

    lax.while_loop(live, pair, (jnp.int32(0), jnp.float32(0.0)) + runs)
    o_ref[...] = acc_ref[...].T


def _stick_breaking(p16, vt, bsz, seq, consts):
    nq = seq // SB_Q
    nk = seq // SB_BLOCK
    return pl.pallas_call(
        _sb_kernel,
        grid=(bsz, nq),
        in_specs=[pl.BlockSpec((SB_Q, GROUP_W), lambda b, i: (b * nq + i, 0)),
                  pl.BlockSpec((seq, GROUP_W), lambda b, i: (b, 1)),
                  pl.BlockSpec((nk, GROUP_W, SB_BLOCK), lambda b, i: (b, 0, 0)),
                  _const_spec((N_HEADS * SB_Q, GROUP_W)),
                  _const_spec((SB_BLOCK, SB_BLOCK)),
                  _const_spec((SUBLANES, SB_BLOCK)),
                  _const_spec((SB_BLOCK, SB_BLOCK)),
                  _const_spec((SB_BLOCK, SB_BLOCK))],
        out_specs=pl.BlockSpec((SB_Q, GROUP_W), lambda b, i: (b * nq + i, 0)),
        out_shape=jax.ShapeDtypeStruct((bsz * seq, GROUP_W), F32),
        scratch_shapes=[pltpu.VMEM((GROUP_W, SB_Q), F32)],
        compiler_params=_params("parallel", "arbitrary"),
        name="stick_breaking",
    )(p16, p16, vt, consts["sb_hm4"], consts["sb_cumt"], consts["sb_neg1"], consts["sb_trib"], consts["sb_trin"])


def _tile4(x):
    return jnp.concatenate([x, x, x, x], axis=0)


def _fold4(x):
    n = x.shape[0] // N_HEADS
    return x[0:n] + x[n:2 * n] + x[2 * n:3 * n] + x[3 * n:4 * n]


def _shifted(cur, prev_ref, first_row):
    sh = jnp.where(first_row, prev_ref[0:1, :], pltpu.roll(cur, 1, axis=0))
    prev_ref[0:1, :] = cur[cur.shape[0] - 1:, :]
    return sh


def _rwkv_kernel(*refs, tb, has_vres):
    if has_vres:
        (cols_ref, vres_ref, vfirst_ref, mu_ref, vmu_ref, w0_ref, w2_ref, a0_ref, a2_ref, g2_ref,
         v0_ref, v2_ref, kk_ref, ka_ref, rk_ref, lnw_ref, lnb_ref,
         hm4_ref, sl_ref, il_ref, eye_ref, ltri_ref, seg_ref,
         o_ref, z_ref, prev_ref, prevv_ref) = refs
    else:
        (cols_ref, mu_ref, w0_ref, w2_ref, a0_ref, a2_ref, g2_ref,
         kk_ref, ka_ref, rk_ref, lnw_ref, lnb_ref,
         hm4_ref, sl_ref, il_ref, eye_ref, ltri_ref, seg_ref,
         o_ref, vout_ref, z_ref, prev_ref) = refs

    @pl.when(pl.program_id(1) == 0)
    def _():
        z_ref[...] = jnp.zeros_like(z_ref)
        prev_ref[...] = jnp.zeros_like(prev_ref)
        if has_vres:
            prevv_ref[...] = jnp.zeros_like(prevv_ref)

    first_row = lax.broadcasted_iota(jnp.int32, (tb, 1), 0) == 0
    cols = cols_ref[...]
    xs = cols + (_shifted(cols, prev_ref, first_row) - cols) * mu_ref[...]
    r = xs[:, 0:GROUP_W]
    k = xs[:, GROUP_W:2 * GROUP_W]
    v = xs[:, 2 * GROUP_W:3 * GROUP_W]
    wa = xs[:, 3 * GROUP_W:3 * GROUP_W + 128]
    gd = xs[:, 3 * GROUP_W + 128:4 * GROUP_W]
    w_log = -_softplus(-(w0_ref[...] + _mm1(jnp.tanh(wa), w2_ref[...]))) - 0.5
    lw = -jnp.exp(w_log)
    a = _sigmoid(a0_ref[...] + _mm1(wa, a2_ref[...]))
    g = _mm1(_sigmoid(gd), g2_ref[...])
    if has_vres:
        vr = vres_ref[...]
        vx = vr + (_shifted(vr, prevv_ref, first_row) - vr) * vmu_ref[...]
        v = v + (vfirst_ref[...] - v) * _sigmoid(v0_ref[...] + _mm1(vx, v2_ref[...]))
    else:
        vout_ref[...] = v
    seg = seg_ref[...]
    kk = k * kk_ref[...]
    kk = kk * lax.rsqrt(jnp.maximum(_mm2(kk * kk, seg), 1e-12))
    k = k * (1.0 + (a - 1.0) * ka_ref[...])
    aa = -kk
    bb = kk * a

    hm4 = hm4_ref[...]
    sl = sl_ref[...]
    il = il_ref[...]
    eye = eye_ref[...]

    w256 = GROUP_W
    ltri = ltri_ref[...]
    each = lambda f, *lists: [f(*xs) for xs in zip(*lists)]

    def chunk_maps(group):
        take = lambda x: [x[ci * RW_CHUNK:(ci + 1) * RW_CHUNK, :] for ci in group]
        rc, kc, vc, ac, bc, lwc = take(r), take(k), take(v), take(aa), take(bb), take(lw)
        c = each(lambda x: _mm2l(ltri, x), lwc)
        c_end = each(lambda x: x[RW_CHUNK - 1:, :], c)
        at4 = each(lambda a_, c_, l_: _tile4(a_ * jnp.exp(c_ - l_)) * hm4, ac, c, lwc)
        rt4 = each(lambda r_, c_: _tile4(r_ * jnp.exp(c_)) * hm4, rc, c)
        at4b = each(lambda x: x.astype(BF16), at4)
        be4b = each(lambda b_, c_, e_: (_tile4(b_ * jnp.exp(e_ - c_)) * hm4).astype(BF16), bc, c, c_end)
        ke4b = each(lambda k_, c_, e_: (_tile4(k_ * jnp.exp(e_ - c_)) * hm4).astype(BF16), kc, c, c_end)
        v4b = each(lambda v_: (_tile4(v_) * hm4).astype(BF16), vc)
        lhs = each(lambda a_, r_: jnp.concatenate([a_, r_.astype(BF16)], axis=0), at4b, rt4)
        rhs = each(lambda b_, k_, c_: jnp.concatenate([_tile4((b_ * jnp.exp(-c_)).astype(BF16)),
                                                        _tile4((k_ * jnp.exp(-c_)).astype(BF16))], axis=0),
                   bc, kc, c)
        nn = each(lambda l_, r_: _dot(l_, r_, NT), lhs, rhs)
        n_ab = each(lambda x: x[:w256, :w256] * sl, nn)
        n_ak = each(lambda x: (x[:w256, w256:] * sl).astype(BF16), nn)
        n_rb = each(lambda x: (x[w256:, :w256] * il).astype(BF16), nn)
        n_rk = each(lambda x: (x[w256:, w256:] * il).astype(BF16), nn)
        yield None
        inv = each(lambda x: eye + x, n_ab)
        pw = each(lambda x: x.astype(BF16), n_ab)
        for _ in range(int(math.log2(RW_CHUNK)) - 1):
            pw = each(lambda x: _dot(x, x).astype(BF16), pw)
            inv = each(lambda i_, p_: i_ + _dot(p_, i_.astype(BF16)), inv, pw)
            yield None
        akv = each(lambda n_, v_: _dot(n_, v_).astype(BF16), n_ak, v4b)
        ugb = each(lambda i_, x_, a_: _dot(i_.astype(BF16), jnp.concatenate([x_, a_], axis=1)).astype(BF16),
                   inv, akv, at4b)
        yield None
        yh = each(_dot, n_rb, ugb)
        y0 = each(lambda y_, n_, v_: y_[:, :w256] + _dot(n_, v_), yh, n_rk, v4b)
        hmat = each(lambda r_, y_: (r_ + y_[:, w256:]).astype(BF16), rt4, yh)
        yield None
        qp = each(lambda b_, u_: _dot(b_, u_, TN), be4b, ugb)
        qm = each(lambda q_, k_, v_: q_[:, :w256] + _dot(k_, v_, TN), qp, ke4b, v4b)
        p = each(lambda e_, q_: (eye * jnp.exp(e_) + q_[:, w256:]).astype(BF16), c_end, qp)
        yield list(zip(y0, hmat, p, qm))

    z = z_ref[...]
    ys = []

    def state_step(chunk_map):
        nonlocal z
        y0, hmat, p, qm = chunk_map
        zb = z.astype(BF16)
        ys.append(_fold4(y0 + _dot(hmat, zb)))
        z = _dot(p, zb) + qm

    n_chunks = tb // RW_CHUNK
    waiting = []
    for first in range(0, n_chunks, RW_GROUP):
        for stage, maps in enumerate(chunk_maps(range(first, min(first + RW_GROUP, n_chunks)))):
            if waiting and stage % 2 == 1:
                state_step(waiting.pop(0))
        for chunk_map in waiting:
            state_step(chunk_map)
        waiting = maps
    for chunk_map in waiting:
        state_step(chunk_map)
    z_ref[...] = z

    y = jnp.concatenate(ys, axis=0)
    inv_n = 1.0 / HEAD_DIM
    mean = _mm2(y, seg) * inv_n
    yc = y - mean
    var = _mm2(yc * yc, seg) * inv_n
    y = yc * lax.rsqrt(var + RWKV_GN_EPS) * lnw_ref[...] + lnb_ref[...]
    bonus = _mm2(r * k * rk_ref[...], seg) * v
    o_ref[...] = (y + bonus) * g


def _rwkv(p32, vfirst, bsz, seq, lw, consts, has_vres, tb=1024):
    nb = seq // tb
    rows = lambda w, j: pl.BlockSpec((tb, w), lambda b, i: (b * nb + i, j))
    vec = _const_spec((1, GROUP_W))
    mat = lambda n: _const_spec((n, GROUP_W))
    big = _const_spec((GROUP_W, GROUP_W))
    cnames = ("hm4", "sl", "il", "eye", "ltri", "seg")
    cspecs = [big, big, big, big, _const_spec((RW_CHUNK, RW_CHUNK)), big]
    cargs = [consts[n] for n in cnames]
    cols_spec = pl.BlockSpec((tb, 4 * GROUP_W), lambda b, i: (b * nb + i, 0))
    out_spec = pl.BlockSpec((tb, GROUP_W), lambda b, i: (b * nb + i, 0))
    out_sds = jax.ShapeDtypeStruct((bsz * seq, GROUP_W), F32)
    scratch = [pltpu.VMEM((GROUP_W, GROUP_W), F32), pltpu.VMEM((SUBLANES, 4 * GROUP_W), F32)]
    if has_vres:
        in_specs = ([cols_spec, rows(GROUP_W, P32_VRES), rows(GROUP_W, 0), _const_spec((1, 4 * GROUP_W)), vec,
                     vec, mat(128), vec, mat(128), mat(RWKV_LORA_G), vec, mat(GROUP_W), vec, vec, vec, vec, vec]
                    + cspecs)
        args = [p32, p32, vfirst, lw["mu"], lw["vmu"], lw["w0"], lw["w2"], lw["a0"], lw["a2"],
                lw["g2"], lw["v0"], lw["v2"], lw["k_k"], lw["k_a"], lw["r_k"], lw["ln_w"], lw["ln_b"]] + cargs
        out_specs, out_shape = out_spec, out_sds
        scratch = scratch + [pltpu.VMEM((SUBLANES, GROUP_W), F32)]
    else:
        in_specs = ([cols_spec, _const_spec((1, 4 * GROUP_W)), vec, mat(128), vec, mat(128), mat(RWKV_LORA_G),
                     vec, vec, vec, vec, vec] + cspecs)
        args = [p32, lw["mu"], lw["w0"], lw["w2"], lw["a0"], lw["a2"], lw["g2"],
                lw["k_k"], lw["k_a"], lw["r_k"], lw["ln_w"], lw["ln_b"]] + cargs
        out_specs, out_shape = [out_spec, out_spec], [out_sds, out_sds]
    return pl.pallas_call(
        functools.partial(_rwkv_kernel, tb=tb, has_vres=has_vres),
        grid=(bsz, nb),
        in_specs=in_specs,
        out_specs=out_specs,
        out_shape=out_shape,
        scratch_shapes=scratch,
        compiler_params=_params("parallel", "arbitrary"),
        name="rwkv7",
    )(*args)


def _constants():
    lane_head = jnp.arange(GROUP_W) // HEAD_DIM
    hm = (lane_head[None, :] == jnp.arange(N_HEADS)[:, None]).astype(F32)
    bd = (lane_head[:, None] == lane_head[None, :])
    seg = bd.astype(BF16)
    pos = jnp.arange(GROUP_W) % RW_CHUNK
    hm4 = bd.astype(F32)
    sl = (bd & (pos[None, :] < pos[:, None])).astype(F32)
    il = (bd & (pos[None, :] <= pos[:, None])).astype(F32)
    eye = jnp.eye(GROUP_W, dtype=F32)
    t = jnp.arange(RW_CHUNK)
    ltri = (t[None, :] <= t[:, None]).astype(BF16)
    j = jnp.arange(SB_BLOCK)
    sb_cumt = -(j[None, :] > j[:, None]).astype(BF16)
    sb_neg1 = -jnp.ones((SUBLANES, SB_BLOCK), BF16)
    sb_hm4 = jnp.repeat(hm, SB_Q, axis=0).astype(BF16)
    visible = j[:, None] < j[None, :]
    sb_trib = visible.astype(BF16)
    sb_trin = jnp.where(visible, 0.0, SB_MASKED).astype(F32)
    return dict(hm=hm, bd=bd.astype(F32), seg=seg, hm4=hm4, sl=sl, il=il, eye=eye, ltri=ltri,
                sb_cumt=sb_cumt, sb_neg1=sb_neg1, sb_hm4=sb_hm4, sb_trib=sb_trib, sb_trin=sb_trin)


def kernel(x, norm_mix_pre, norm_mix_post, norm_ffn_pre, norm_ffn_post, w_in_first, w_in_rest, w_out,
           s5_a_re, s5_a_im, s5_log_dt, s5_b_re, s5_b_im, s5_c_re, s5_c_im, s5_d, s5_glu_w1, s5_glu_w2,
           rw_mu, rw_vres_mu, rw_w0, rw_w2, rw_a0, rw_a2, rw_g2, rw_v0, rw_v2,
           rw_k_k, rw_k_a, rw_r_k, rw_ln_w, rw_ln_b, w_up, w_down):
    bsz, seq, _ = x.shape
    consts = _constants()
    ret_tabs = _retention_tables(seq)
    x2 = x.reshape(bsz * seq, D_MODEL)
    row = lambda vec: vec.reshape(1, -1)
    zpad = lambda m, n: jnp.pad(m, ((0, n - m.shape[0]), (0, 0)))
    v_first = None
    for l in range(DEPTH):
        w_in = w_in_first if l == 0 else w_in_rest[l - 1]
        parts = [w_in[:, OFF_RW:N_IN0], w_in[:, OFF_RET:OFF_SB], w_in[:, :OFF_RET]]
        if l > 0:
            parts.append(jnp.pad(w_in[:, N_IN0:], ((0, 0), (0, GROUP_W - RWKV_LORA_V))))
        n32 = sum(p.shape[1] for p in parts)
        w_cat = jnp.concatenate(parts + [w_in[:, OFF_SB:OFF_RW]], axis=1).astype(BF16)
        p32, p16, vt = _in_proj(x2, row(norm_mix_pre[l]), w_cat, n32, 2 * GROUP_W)

        wb, wc, pw = _s5_tables(s5_a_re[l], s5_a_im[l], s5_log_dt[l], s5_b_re[l], s5_b_im[l],
                                s5_c_re[l], s5_c_im[l])
        out_s5 = _s5(p32, bsz, seq, wb, wc, pw, row(s5_d[l]),
                     s5_glu_w1[l].astype(BF16), s5_glu_w2[l].astype(BF16))
        out_ret = _retention(p32, bsz, seq, ret_tabs, consts)
        out_sb = _stick_breaking(p16, vt, bsz, seq, consts)

        lw = dict(
            mu=row(rw_mu[l]), w0=row(rw_w0[l]), a0=row(rw_a0[l]),
            w2=zpad(rw_w2[l], 128).astype(BF16),
            a2=jnp.pad(rw_a2[l], ((RWKV_LORA_W, 0), (0, 0))).astype(BF16),
            g2=rw_g2[l].astype(BF16),
            k_k=row(rw_k_k[l]), k_a=row(rw_k_a[l]), r_k=row(rw_r_k[l]),
            ln_w=row(rw_ln_w[l]), ln_b=row(rw_ln_b[l]))
        if l == 0:
            out_rw, v_first = _rwkv(p32, None, bsz, seq, lw, consts, False)
        else:
            lw.update(vmu=row(jnp.pad(rw_vres_mu[l - 1], (0, GROUP_W - RWKV_LORA_V))),
                      v0=row(rw_v0[l - 1]), v2=zpad(rw_v2[l - 1], GROUP_W).astype(BF16))
            out_rw = _rwkv(p32, v_first, bsz, seq, lw, consts, True)

        x2 = _post((out_s5, out_ret, out_rw, out_sb), x2, w_out[l].astype(BF16), row(norm_mix_post[l]),
                   row(norm_ffn_pre[l]), w_up[l].astype(BF16), w_down[l].astype(BF16), row(norm_ffn_post[l]))
    return x2.reshape(bsz, seq, D_MODEL)
```
